```python
import math, functools
import jax, jax.numpy as jnp
from jax import lax
import numpy as np

D_MODEL = 1024
BATCH = 2
SEQ = 8192
DEPTH = 2
DEC_BATCH = 128
DEC_SEQ = 4
PAST_LEN = 2048
PAGE_SIZE = 128

LRU_WIDTH = 1024
LRU_BLOCKS = 8
LRU_BLOCK_W = LRU_WIDTH // LRU_BLOCKS
LRU_CONV = 4
LRU_C = 8.0
CONF_CH = 1024
CONF_CONV = 31
N_HEADS = 8
N_KV = 2
HPG = N_HEADS // N_KV
HEAD_DIM = 128
CMP_LEN = 32
CMP_STRIDE = 16
CMP_HID = 128
SEL_BLOCK = 64
N_SEL = 16
CH_PER_SEL = SEL_BLOCK // CMP_STRIDE
WINDOW = 512
Q_BLOCK = 128
N_NSA_BRANCH = 3
ROPE_THETA = 10000.0
N_BRANCH = 3
D_FF = 3072
FFN_CONV = 3
EPS = 1e-6
NEG = -1e30
BIG = 1e9

IN_SIZES = (LRU_WIDTH, LRU_WIDTH, 2 * CONF_CH, N_HEADS * HEAD_DIM,
            2 * N_KV * HEAD_DIM, 2 * N_KV * HEAD_DIM, 2 * N_KV * HEAD_DIM,
            N_NSA_BRANCH * N_HEADS, N_BRANCH * D_MODEL)
N_IN = sum(IN_SIZES)

kernel_name = 'hybrid_rglru_conformer_nsa_step'

F32 = jnp.float32


def rmsnorm(x, g):
    xf = x.astype(F32)
    y = xf * lax.rsqrt(jnp.mean(xf * xf, axis=-1, keepdims=True) + EPS)
    return (y * g.astype(F32)).astype(x.dtype)


def layernorm(x, g, b):
    xf = x.astype(F32)
    mu = jnp.mean(xf, axis=-1, keepdims=True)
    var = jnp.mean(jnp.square(xf - mu), axis=-1, keepdims=True)
    return ((xf - mu) * lax.rsqrt(var + EPS) * g.astype(F32) + b.astype(F32)).astype(x.dtype)


def rope(x, pos):
    half = HEAD_DIM // 2
    inv = ROPE_THETA ** (-jnp.arange(half, dtype=F32) / half)
    ang = pos.astype(F32)[:, None] * inv[None, :]
    cos = jnp.cos(ang)[None, :, None, :]
    sin = jnp.sin(ang)[None, :, None, :]
    xf = x.astype(F32)
    x1, x2 = xf[..., :half], xf[..., half:]
    return jnp.concatenate([x1 * cos - x2 * sin, x2 * cos + x1 * sin], axis=-1).astype(x.dtype)


def rope_kv(kv, pos):
    return jnp.stack([rope(kv[:, :, 0], pos), kv[:, :, 1]], axis=2)


def causal_dwconv(x, prev, w, b):
    k = w.shape[0]
    xx = jnp.concatenate([prev.astype(x.dtype), x], axis=1)
    y = lax.conv_general_dilated(xx, w[:, None, :].astype(x.dtype), window_strides=(1,), padding='VALID',
                                 dimension_numbers=('NWC', 'WIO', 'NWC'), feature_group_count=x.shape[-1])
    return y + b.astype(x.dtype), xx[:, xx.shape[1] - (k - 1):]


def masked_softmax(s, valid):
    p = jax.nn.softmax(jnp.where(valid, s, NEG), axis=-1)
    return jnp.where(valid, p, 0.0)


def rglru_branch(xr, yr, h0, buf, conv_w, conv_b, gate_w, gate_b, lam):
    B, T, W = xr.shape
    xc, buf_new = causal_dwconv(xr, buf, conv_w, conv_b)
    xf = xc.astype(F32)
    xb = xf.reshape(B, T, LRU_BLOCKS, LRU_BLOCK_W)
    g = jnp.einsum('btni,knij->kbtnj', xb, gate_w.astype(F32)) + gate_b.astype(F32)[:, None, None]
    g = jax.nn.sigmoid(g).reshape(2, B, T, W)
    log_a = LRU_C * g[1] * jax.nn.log_sigmoid(lam.astype(F32))
    a = jnp.exp(log_a)
    b = xf * g[0] * jnp.sqrt(-jnp.expm1(2.0 * log_a))

    def step(h, ab):
        h = ab[0] * h + ab[1]
        return h, h

    hT, hs = lax.scan(step, h0.astype(F32), (jnp.moveaxis(a, 1, 0), jnp.moveaxis(b, 1, 0)))
    out = jnp.moveaxis(hs, 0, 1) * jax.nn.gelu(yr.astype(F32))
    return out.astype(xr.dtype), hT.astype(xr.dtype), buf_new


def conformer_branch(u, buf, conv_w, conv_b, ln_g, ln_b):
    glu = u[..., :CONF_CH] * jax.nn.sigmoid(u[..., CONF_CH:])
    c, buf_new = causal_dwconv(glu, buf, conv_w, conv_b)
    n = layernorm(c, ln_g, ln_b)
    return jax.nn.silu(n.astype(F32)).astype(u.dtype), buf_new


def compress_kv(kv, pe, w1, w2):
    B, T = kv.shape[:2]
    n = T // CMP_STRIDE
    ch = kv[:, :n * CMP_STRIDE].reshape(B, n, CMP_STRIDE, 2, N_KV, HEAD_DIM)
    pe_t = jnp.moveaxis(pe, 0, 1)[:, :, None, :]
    first = jnp.einsum('bclkgd,klde->bckge', ch + pe_t[:CMP_STRIDE], w1[:, :CMP_STRIDE])
    second = jnp.einsum('bclkgd,klde->bckge', ch + pe_t[CMP_STRIDE:], w1[:, CMP_STRIDE:])
    hid = jax.nn.gelu(first[:, :-1] + second[:, 1:])
    out = jnp.einsum('bnkge,kef->bnkgf', hid, w2)
    return out[:, :, 0], out[:, :, 1]


def nsa_core(q, qpos, kc, vc, ksb, vsb, kw, vw, kwpos, gates):
    B, Q = q.shape[:2]
    NC = kc.shape[1]
    NS = ksb.shape[1]
    s_c = jnp.einsum('bqghd,bngd->bqghn', q, kc).astype(F32)
    ok_c = (jnp.arange(NC) * CMP_STRIDE + CMP_LEN - 1)[None, :] <= qpos[:, None]
    p_c = masked_softmax(s_c, ok_c[None, :, None, None, :])
    o_c = jnp.einsum('bqghn,bngd->bqghd', p_c, vc.astype(F32))
    imp = jnp.sum(p_c, axis=3)
    pad4 = ((0, 0), (0, 0), (0, 0))
    chunk = jnp.pad(imp, pad4 + ((0, 1),)) + jnp.pad(imp, pad4 + ((1, 0),))
    chunk = jnp.pad(chunk, pad4 + ((0, NS * CH_PER_SEL - (NC + 1)),))
    imp_s = chunk.reshape(B, Q, N_KV, NS, CH_PER_SEL).sum(-1)
    blk = jnp.arange(NS)[None, :]
    cur = (qpos // SEL_BLOCK)[:, None]
    visible = blk * SEL_BLOCK <= qpos[:, None]
    forced = (blk == 0) | (blk == cur) | (blk == cur - 1)
    score = jnp.where(forced[None, :, None, :], BIG, imp_s)
    score = jnp.where(visible[None, :, None, :], score, -BIG)
    top, idx = lax.top_k(score, min(N_SEL, NS))
    sel_ok = top > -0.5 * BIG
    ksr = jnp.moveaxis(ksb, 3, 1)
    vsr = jnp.moveaxis(vsb, 3, 1)
    bi = jnp.arange(B)[:, None, None, None]
    gi = jnp.arange(N_KV)[None, None, :, None]
    kg = ksr[bi, gi, idx]
    vg = vsr[bi, gi, idx]
    K = idx.shape[-1]
    s_s = jnp.einsum('bqghd,bqgksd->bqghks', q, kg).astype(F32).reshape(B, Q, N_KV, HPG, K * SEL_BLOCK)
    kpos = idx[..., None] * SEL_BLOCK + jnp.arange(SEL_BLOCK)
    ok_s = (kpos <= qpos[None, :, None, None, None]) & sel_ok[..., None]
    p_s = masked_softmax(s_s, ok_s.reshape(B, Q, N_KV, 1, K * SEL_BLOCK)).reshape(B, Q, N_KV, HPG, K, SEL_BLOCK)
    o_s = jnp.einsum('bqghks,bqgksd->bqghd', p_s, vg.astype(F32))
    s_w = jnp.einsum('bqghd,bkgd->bqghk', q, kw).astype(F32)
    ok_w = (kwpos[None, :] <= qpos[:, None]) & (kwpos[None, :] > qpos[:, None] - WINDOW) & (kwpos[None, :] >= 0)
    p_w = masked_softmax(s_w, ok_w[None, :, None, None, :])
    o_w = jnp.einsum('bqghk,bkgd->bqghd', p_w, vw.astype(F32))
    g = gates.astype(F32)
    o = g[..., 0:1] * o_c + g[..., 1:2] * o_s + g[..., 2:3] * o_w
    return o.astype(q.dtype)


def nsa_prompt(q, kvc, kvs, kvw, gates, pe, w1, w2):
    B, S = q.shape[:2]
    kc, vc = compress_kv(kvc, pe, w1, w2)
    ksb = kvs[:, :, 0].reshape(B, S // SEL_BLOCK, SEL_BLOCK, N_KV, HEAD_DIM)
    vsb = kvs[:, :, 1].reshape(B, S // SEL_BLOCK, SEL_BLOCK, N_KV, HEAD_DIM)
    kwp = jnp.pad(kvw, ((0, 0), (WINDOW, 0), (0, 0), (0, 0), (0, 0)))
    span = WINDOW + Q_BLOCK

    def one_block(j):
        start = j * Q_BLOCK
        qb = lax.dynamic_slice_in_dim(q, start, Q_BLOCK, axis=1)
        gb = lax.dynamic_slice_in_dim(gates, start, Q_BLOCK, axis=1)
        wb = lax.dynamic_slice_in_dim(kwp, start, span, axis=1)
        qpos = start + jnp.arange(Q_BLOCK)
        kwpos = start - WINDOW + jnp.arange(span)
        return nsa_core(qb, qpos, kc, vc, ksb, vsb, wb[:, :, 0], wb[:, :, 1], kwpos, gb)

    o = lax.map(one_block, jnp.arange(S // Q_BLOCK))
    o = jnp.moveaxis(o, 0, 1).reshape(B, S, N_HEADS * HEAD_DIM)
    return o, kvw[:, S - min(WINDOW, S):]


def nsa_sample(q, kvc, kvs, kvw, gates, pe, w1, w2, cmp_pool, slc_pool, win_buf, page_table):
    DB, TS = q.shape[:2]
    past = page_table.shape[1] * PAGE_SIZE
    T = past + TS

    def gather(pool):
        return pool[page_table].reshape(DB, past, 2, N_KV, HEAD_DIM)

    cmp_all = jnp.concatenate([gather(cmp_pool), kvc], axis=1)
    kc, vc = compress_kv(cmp_all, pe, w1, w2)
    ns = -(-T // SEL_BLOCK)
    slc_all = jnp.concatenate([gather(slc_pool), kvs], axis=1)
    slc_all = jnp.pad(slc_all, ((0, 0), (0, ns * SEL_BLOCK - T), (0, 0), (0, 0), (0, 0)))
    ksb = slc_all[:, :, 0].reshape(DB, ns, SEL_BLOCK, N_KV, HEAD_DIM)
    vsb = slc_all[:, :, 1].reshape(DB, ns, SEL_BLOCK, N_KV, HEAD_DIM)
    wbuf = win_buf.shape[1]
    win_all = jnp.concatenate([win_buf, kvw], axis=1)
    kwpos = past - wbuf + jnp.arange(wbuf + TS)
    qpos = past + jnp.arange(TS)
    o = nsa_core(q, qpos, kc, vc, ksb, vsb, win_all[:, :, 0], win_all[:, :, 1], kwpos, gates)
    return o.reshape(DB, TS, N_HEADS * HEAD_DIM), win_all[:, win_all.shape[1] - min(WINDOW, T):]


def token_mixer(h, lp, pos, lru_h0, lru_buf, conf_buf, attend):
    B, T, _ = h.shape
    z = h @ lp['w_in']
    offs = np.cumsum(IN_SIZES)[:-1].tolist()
    xr, yr, u, q, kvc, kvs, kvw, ng, mg = jnp.split(z, offs, axis=-1)
    a_out, lru_h, lru_buf_new = rglru_branch(xr, yr, lru_h0, lru_buf, lp['lru_conv_w'], lp['lru_conv_b'],
                                             lp['lru_gate_w'], lp['lru_gate_b'], lp['lru_lambda'])
    b_out, conf_buf_new = conformer_branch(u, conf_buf, lp['conf_conv_w'], lp['conf_conv_b'],
                                           lp['conf_ln_g'], lp['conf_ln_b'])
    q = rope(q.reshape(B, T, N_HEADS, HEAD_DIM), pos) * HEAD_DIM ** -0.5
    q = q.reshape(B, T, N_KV, HPG, HEAD_DIM)
    kvc = rope_kv(kvc.reshape(B, T, 2, N_KV, HEAD_DIM), pos)
    kvs = rope_kv(kvs.reshape(B, T, 2, N_KV, HEAD_DIM), pos)
    kvw = rope_kv(kvw.reshape(B, T, 2, N_KV, HEAD_DIM), pos)
    gates = jax.nn.sigmoid(ng.astype(F32)).reshape(B, T, N_KV, HPG, N_NSA_BRANCH)
    c_out, win_new = attend(q, kvc, kvs, kvw, gates)
    br = jnp.stack([a_out, b_out, c_out.astype(h.dtype)], axis=2)
    proj = jnp.einsum('btnw,nwd->btnd', br, lp['w_branch'])
    mgate = jax.nn.sigmoid(mg.astype(F32)).reshape(B, T, N_BRANCH, D_MODEL)
    merged = jnp.sum(mgate * proj.astype(F32), axis=2).astype(h.dtype)
    return merged @ lp['w_out'], (kvc, kvs, win_new, lru_h, lru_buf_new, conf_buf_new)


def conv_ffn(h, lp, buf):
    gu = h @ lp['ffn_w_in']
    g, u = gu[..., :D_FF], gu[..., D_FF:]
    c, buf_new = causal_dwconv(g, buf, lp['ffn_conv_w'], lp['ffn_conv_b'])
    act = (jax.nn.gelu(c.astype(F32)) * u.astype(F32)).astype(h.dtype)
    return act @ lp['ffn_w_out'], buf_new


def layer(x, lp, pos, lru_h0, lru_buf, conf_buf, ffn_buf, attend):
    m, mix_state = token_mixer(rmsnorm(x, lp['norm_pre_mix']), lp, pos, lru_h0, lru_buf, conf_buf, attend)
    x = x + rmsnorm(m, lp['norm_post_mix'])
    f, ffn_buf_new = conv_ffn(rmsnorm(x, lp['norm_pre_ffn']), lp, ffn_buf)
    x = x + rmsnorm(f, lp['norm_post_ffn'])
    return x, mix_state + (ffn_buf_new,)


def setup_inputs(seed: int = 0) -> dict:
    key = jax.random.key(seed)
    ks = iter(jax.random.split(key, 40))

    def nrm(shape, scale):
        return scale * jax.random.normal(next(ks), shape, F32)

    n_pages = PAST_LEN // PAGE_SIZE
    used = DEC_BATCH * n_pages
    n_pool = used + max(1, used // 4)
    win_buf = min(WINDOW, PAST_LEN)
    kv_row = (2, N_KV, HEAD_DIM)
    page_table = jax.random.permutation(next(ks), n_pool)[:used].reshape(DEC_BATCH, n_pages).astype(jnp.int32)
    u = jax.random.uniform(next(ks), (DEPTH, LRU_WIDTH), F32, 0.9, 0.999) ** (1.0 / LRU_C)
    lru_lambda = jnp.log(u) - jnp.log1p(-u)
    return {
        'x_prompt': nrm((BATCH, SEQ, D_MODEL), 1.0),
        'x_sample': nrm((DEC_BATCH, DEC_SEQ, D_MODEL), 1.0),
        'cache_cmp_kv': nrm((DEPTH, n_pool, PAGE_SIZE) + kv_row, 1.0),
        'cache_slc_kv': nrm((DEPTH, n_pool, PAGE_SIZE) + kv_row, 1.0),
        'cache_win_kv': nrm((DEPTH, DEC_BATCH, win_buf) + kv_row, 1.0),
        'state_lru_h': nrm((DEPTH, DEC_BATCH, LRU_WIDTH), 0.5),
        'state_lru_conv': nrm((DEPTH, DEC_BATCH, LRU_CONV - 1, LRU_WIDTH), 1.0),
        'state_conf_conv': nrm((DEPTH, DEC_BATCH, CONF_CONV - 1, CONF_CH), 0.5),
        'state_ffn_conv': nrm((DEPTH, DEC_BATCH, FFN_CONV - 1, D_FF), 1.0),
        'page_table': page_table,
        'norm_pre_mix': 1.0 + nrm((DEPTH, D_MODEL), 0.05),
        'norm_post_mix': 1.0 + nrm((DEPTH, D_MODEL), 0.05),
        'norm_pre_ffn': 1.0 + nrm((DEPTH, D_MODEL), 0.05),
        'norm_post_ffn': 1.0 + nrm((DEPTH, D_MODEL), 0.05),
        'w_in': nrm((DEPTH, D_MODEL, N_IN), D_MODEL ** -0.5),
        'lru_conv_w': nrm((DEPTH, LRU_CONV, LRU_WIDTH), LRU_CONV ** -0.5),
        'lru_conv_b': nrm((DEPTH, LRU_WIDTH), 0.01),
        'lru_gate_w': nrm((DEPTH, 2, LRU_BLOCKS, LRU_BLOCK_W, LRU_BLOCK_W), LRU_BLOCK_W ** -0.5),
        'lru_gate_b': nrm((DEPTH, 2, LRU_BLOCKS, LRU_BLOCK_W), 0.01),
        'lru_lambda': lru_lambda,
        'conf_conv_w': nrm((DEPTH, CONF_CONV, CONF_CH), CONF_CONV ** -0.5),
        'conf_conv_b': nrm((DEPTH, CONF_CH), 0.01),
        'conf_ln_g': 1.0 + nrm((DEPTH, CONF_CH), 0.05),
        'conf_ln_b': nrm((DEPTH, CONF_CH), 0.01),
        'cmp_pe': nrm((DEPTH, 2, CMP_LEN, HEAD_DIM), 0.1),
        'cmp_w1': nrm((DEPTH, 2, CMP_LEN, HEAD_DIM, CMP_HID), (CMP_LEN * HEAD_DIM) ** -0.5),
        'cmp_w2': nrm((DEPTH, 2, CMP_HID, HEAD_DIM), CMP_HID ** -0.5),
        'w_branch': nrm((DEPTH, N_BRANCH, LRU_WIDTH, D_MODEL), LRU_WIDTH ** -0.5),
        'w_out': nrm((DEPTH, D_MODEL, D_MODEL), D_MODEL ** -0.5),
        'ffn_w_in': nrm((DEPTH, D_MODEL, 2 * D_FF), D_MODEL ** -0.5),
        'ffn_conv_w': nrm((DEPTH, FFN_CONV, D_FF), FFN_CONV ** -0.5),
        'ffn_conv_b': nrm((DEPTH, D_FF), 0.01),
        'ffn_w_out': nrm((DEPTH, D_FF, D_MODEL), D_FF ** -0.5),
    }


def reference(x_prompt, x_sample, cache_cmp_kv, cache_slc_kv, cache_win_kv, state_lru_h, state_lru_conv,
              state_conf_conv, state_ffn_conv, page_table, norm_pre_mix, norm_post_mix, norm_pre_ffn,
              norm_post_ffn, w_in, lru_conv_w, lru_conv_b, lru_gate_w, lru_gate_b, lru_lambda, conf_conv_w,
              conf_conv_b, conf_ln_g, conf_ln_b, cmp_pe, cmp_w1, cmp_w2, w_branch, w_out, ffn_w_in,
              ffn_conv_w, ffn_conv_b, ffn_w_out):
    B, S, _ = x_prompt.shape
    DB, TS, _ = x_sample.shape
    past = page_table.shape[1] * PAGE_SIZE
    pos_p = jnp.arange(S)
    pos_s = past + jnp.arange(TS)
    xp, xs = x_prompt, x_sample
    st_p, st_s = [], []
    for l in range(DEPTH):
        lp = {'norm_pre_mix': norm_pre_mix[l], 'norm_post_mix': norm_post_mix[l],
              'norm_pre_ffn': norm_pre_ffn[l], 'norm_post_ffn': norm_post_ffn[l], 'w_in': w_in[l],
              'lru_conv_w': lru_conv_w[l], 'lru_conv_b': lru_conv_b[l], 'lru_gate_w': lru_gate_w[l],
              'lru_gate_b': lru_gate_b[l], 'lru_lambda': lru_lambda[l], 'conf_conv_w': conf_conv_w[l],
              'conf_conv_b': conf_conv_b[l], 'conf_ln_g': conf_ln_g[l], 'conf_ln_b': conf_ln_b[l],
              'w_branch': w_branch[l], 'w_out': w_out[l], 'ffn_w_in': ffn_w_in[l],
              'ffn_conv_w': ffn_conv_w[l], 'ffn_conv_b': ffn_conv_b[l], 'ffn_w_out': ffn_w_out[l]}
        attend_p = functools.partial(nsa_prompt, pe=cmp_pe[l], w1=cmp_w1[l], w2=cmp_w2[l])
        xp, sp = layer(xp, lp, pos_p,
                       jnp.zeros((B, LRU_WIDTH), xp.dtype),
                       jnp.zeros((B, LRU_CONV - 1, LRU_WIDTH), xp.dtype),
                       jnp.zeros((B, CONF_CONV - 1, CONF_CH), xp.dtype),
                       jnp.zeros((B, FFN_CONV - 1, D_FF), xp.dtype), attend_p)
        attend_s = functools.partial(nsa_sample, pe=cmp_pe[l], w1=cmp_w1[l], w2=cmp_w2[l],
                                     cmp_pool=cache_cmp_kv[l], slc_pool=cache_slc_kv[l],
                                     win_buf=cache_win_kv[l], page_table=page_table)
        xs, ss = layer(xs, lp, pos_s, state_lru_h[l], state_lru_conv[l], state_conf_conv[l],
                       state_ffn_conv[l], attend_s)
        st_p.append(sp)
        st_s.append(ss)
    new_cmp_kv_prompt = jnp.stack([s[0] for s in st_p])
    new_cmp_kv_sample = jnp.stack([s[0] for s in st_s])
    new_slc_kv_prompt = jnp.stack([s[1] for s in st_p])
    new_slc_kv_sample = jnp.stack([s[1] for s in st_s])
    new_win_kv_prompt = jnp.stack([s[2] for s in st_p])
    new_win_kv_sample = jnp.stack([s[2] for s in st_s])
    new_lru_h_prompt = jnp.stack([s[3] for s in st_p])
    new_lru_h_sample = jnp.stack([s[3] for s in st_s])
    new_lru_conv_prompt = jnp.stack([s[4] for s in st_p])
    new_lru_conv_sample = jnp.stack([s[4] for s in st_s])
    new_conf_conv_prompt = jnp.stack([s[5] for s in st_p])
    new_conf_conv_sample = jnp.stack([s[5] for s in st_s])
    new_ffn_conv_prompt = jnp.stack([s[6] for s in st_p])
    new_ffn_conv_sample = jnp.stack([s[6] for s in st_s])
    return (xp, xs, new_cmp_kv_prompt, new_cmp_kv_sample, new_slc_kv_prompt, new_slc_kv_sample,
            new_win_kv_prompt, new_win_kv_sample, new_lru_h_prompt, new_lru_h_sample,
            new_lru_conv_prompt, new_lru_conv_sample, new_conf_conv_prompt, new_conf_conv_sample,
            new_ffn_conv_prompt, new_ffn_conv_sample)
```

```python
import functools

import jax
import jax.numpy as jnp
import numpy as np
from jax import lax
from jax.experimental import pallas as pl
from jax.experimental.pallas import tpu as pltpu

F32 = jnp.float32
BF16 = jnp.bfloat16

D_MODEL = 1024
PAGE_SIZE = 128
LRU_WIDTH = 1024
LRU_BLOCKS = 8
LRU_BLOCK_W = LRU_WIDTH // LRU_BLOCKS
LRU_CONV = 4
LRU_C = 8.0
CONF_CH = 1024
CONF_CONV = 31
N_HEADS = 8
N_KV = 2
HPG = N_HEADS // N_KV
HEAD_DIM = 128
CMP_LEN = 32
CMP_STRIDE = 16
CMP_HID = 128
SEL_BLOCK = 64
N_SEL = 16
CH_PER_SEL = SEL_BLOCK // CMP_STRIDE
WINDOW = 512
Q_BLOCK = 128
N_NSA_BRANCH = 3
ROPE_THETA = 10000.0
N_BRANCH = 3
D_FF = 3072
FFN_CONV = 3
EPS = 1e-6
NEG = -1e30
BIG = 1e9

LANES = 128
SUBLANES = 8
KV_W = 2 * N_KV * HEAD_DIM
COL_TILE = 512
SEL_KEY_TILE = 512
SAMPLE_Q_PAD = 8
VMEM_LIMIT = 56 * 1024 * 1024

_T_XR, _T_YR, _T_U, _T_Q, _T_KVC, _T_KVS, _T_KVW, _T_MG, _T_NG = 0, 2, 4, 8, 10, 11, 12, 13, 19
_N_IN_TILES = 20
_NG_OFF = LRU_WIDTH * 2 + 2 * CONF_CH + N_HEADS * HEAD_DIM + 3 * KV_W
_NG_W = N_NSA_BRANCH * N_HEADS


def _params(*sem):
    return pltpu.CompilerParams(dimension_semantics=sem, vmem_limit_bytes=VMEM_LIMIT)


def _gelu(x):
    return 0.5 * x * (1.0 + jnp.tanh(0.7978845608028654 * (x + 0.044715 * (x * x * x))))


def _sigmoid(x):
    return 1.0 / (1.0 + jnp.exp(-x))


def _rms(x, g):
    return x * lax.rsqrt(jnp.mean(x * x, axis=-1, keepdims=True) + EPS) * g


def _rope_heads(z, cos, sin, nheads):
    outs = []
    for h in range(nheads):
        xh = z[:, h * HEAD_DIM:(h + 1) * HEAD_DIM]
        outs.append(xh * cos + pltpu.roll(xh, HEAD_DIM // 2, axis=1) * sin)
    return outs


def _in_proj_kernel(x_ref, g_ref, w_ref, cos_ref, sin_ref,
                    xr_ref, yr_ref, u_ref, q_ref, kvc_ref, kvs_ref, kvw_ref, kvsb_ref, kvwb_ref,
                    mg_ref, ng_ref, h_ref):
    j = pl.program_id(1)

    @pl.when(j == 0)
    def _():
        h_ref[...] = _rms(x_ref[...], g_ref[...]).astype(BF16)

    z = jnp.dot(h_ref[...], w_ref[...], preferred_element_type=F32)

    @pl.when(j < _T_YR)
    def _():
        xr_ref[...] = z

    @pl.when((j >= _T_YR) & (j < _T_U))
    def _():
        yr_ref[...] = z

    @pl.when((j >= _T_U) & (j < _T_Q))
    def _():
        u_ref[...] = z

    @pl.when((j >= _T_Q) & (j < _T_KVC))
    def _():
        heads = _rope_heads(z, cos_ref[...], sin_ref[...], COL_TILE // HEAD_DIM)
        q_ref[...] = (jnp.concatenate(heads, axis=1) * HEAD_DIM ** -0.5).astype(BF16)

    def kv_rows():
        heads = _rope_heads(z, cos_ref[...], sin_ref[...], N_KV)
        return jnp.concatenate(heads + [z[:, N_KV * HEAD_DIM:]], axis=1)

    @pl.when(j == _T_KVC)
    def _():
        kvc_ref[...] = kv_rows()

    @pl.when(j == _T_KVS)
    def _():
        kv = kv_rows()
        kvs_ref[...] = kv
        kvsb_ref[...] = kv.astype(BF16)

    @pl.when(j == _T_KVW)
    def _():
        kv = kv_rows()
        kvw_ref[...] = kv
        kvwb_ref[...] = kv.astype(BF16)

    @pl.when((j >= _T_MG) & (j < _T_NG))
    def _():
        mg_ref[...] = _sigmoid(z)

    @pl.when(j == _T_NG)
    def _():
        ng_ref[...] = _sigmoid(z[:, :N_KV * LANES])


def _in_proj(x, g, w, cos, sin, tm):
    m = x.shape[0]
    nper = cos.shape[0] // tm

    def cols(t0, n):
        return lambda i, j: (i, jnp.clip(j - t0, 0, n - 1))

    tile = lambda t0, n: pl.BlockSpec((tm, COL_TILE), cols(t0, n))
    out_shape = [jax.ShapeDtypeStruct((m, LRU_WIDTH), F32), jax.ShapeDtypeStruct((m, LRU_WIDTH), F32),
                 jax.ShapeDtypeStruct((m, 2 * CONF_CH), F32), jax.ShapeDtypeStruct((m, N_HEADS * HEAD_DIM), BF16),
                 jax.ShapeDtypeStruct((m, KV_W), F32), jax.ShapeDtypeStruct((m, KV_W), F32),
                 jax.ShapeDtypeStruct((m, KV_W), F32), jax.ShapeDtypeStruct((m, KV_W), BF16),
                 jax.ShapeDtypeStruct((m, KV_W), BF16), jax.ShapeDtypeStruct((m, N_BRANCH * D_MODEL), F32),
                 jax.ShapeDtypeStruct((m, N_KV * LANES), F32)]
    out_specs = [tile(_T_XR, 2), tile(_T_YR, 2), tile(_T_U, 4), tile(_T_Q, 2), tile(_T_KVC, 1), tile(_T_KVS, 1),
                 tile(_T_KVW, 1), tile(_T_KVS, 1), tile(_T_KVW, 1), tile(_T_MG, 6),
                 pl.BlockSpec((tm, N_KV * LANES), lambda i, j: (i, 0))]
    return pl.pallas_call(
        _in_proj_kernel, name="in_proj",
        grid=(m // tm, _N_IN_TILES),
        in_specs=[pl.BlockSpec((tm, D_MODEL), lambda i, j: (i, 0)),
                  pl.BlockSpec((1, D_MODEL), lambda i, j: (0, 0)),
                  pl.BlockSpec((D_MODEL, COL_TILE), lambda i, j: (0, j)),
                  pl.BlockSpec((tm, HEAD_DIM), lambda i, j: (i % nper, 0)),
                  pl.BlockSpec((tm, HEAD_DIM), lambda i, j: (i % nper, 0))],
        out_specs=out_specs, out_shape=out_shape,
        scratch_shapes=[pltpu.VMEM((tm, D_MODEL), BF16)],
        compiler_params=_params("arbitrary", "arbitrary"),
    )(x, g, w, cos, sin)


def _halo_pad(halo):
    return (-halo) % SUBLANES


def _lru_kernel(xr_ref, yr_ref, buf0_ref, h0_ref, cw_ref, cb_ref, gw_ref, gb_ref, lsl_ref,
                out_ref, hnew_ref, bufnew_ref, xbuf, a_s, b_s, hcar, *, R, tt):
    halo = (LRU_CONV - 1) * R
    p0 = _halo_pad(halo)

    @pl.when(pl.program_id(1) == 0)
    def _():
        xbuf[p0:p0 + halo, :] = buf0_ref[0]
        hcar[...] = h0_ref[0]

    xbuf[p0 + halo:p0 + halo + tt, :] = xr_ref[...]
    xc = jnp.broadcast_to(cb_ref[...], (tt, LRU_WIDTH))
    for k in range(LRU_CONV):
        xc = xc + cw_ref[k:k + 1, :] * xbuf[p0 + k * R:p0 + k * R + tt, :]
    gi, gr = [], []
    for n in range(LRU_BLOCKS):
        xb = xc[:, n * LRU_BLOCK_W:(n + 1) * LRU_BLOCK_W].astype(BF16)
        gi.append(jnp.dot(xb, gw_ref[0, n], preferred_element_type=F32))
        gr.append(jnp.dot(xb, gw_ref[1, n], preferred_element_type=F32))
    gi = _sigmoid(jnp.concatenate(gi, axis=1) + gb_ref[0:1, :])
    gr = _sigmoid(jnp.concatenate(gr, axis=1) + gb_ref[1:2, :])
    log_a = LRU_C * gr * lsl_ref[...]
    a_s[...] = jnp.exp(log_a)
    th = jnp.tanh(log_a)
    b_s[...] = xc * gi * jnp.sqrt(-2.0 * th / (1.0 - th))

    def step(s, h):
        rows = pl.ds(pl.multiple_of(s * R, R), R)
        h = a_s[rows, :] * h + b_s[rows, :]
        b_s[rows, :] = h
        return h

    h = lax.fori_loop(0, tt // R, step, hcar[...], unroll=8 if tt // R >= 8 else True)
    hcar[...] = h
    hnew_ref[0] = h
    out_ref[...] = b_s[...] * _gelu(yr_ref[...])
    tail = xbuf[p0 + tt:p0 + tt + halo, :]
    bufnew_ref[0] = tail
    xbuf[p0:p0 + halo, :] = tail


def _lru(xr, yr, buf0, h0, cw, cb, gw, gb, lsl, *, n_seq, R, tt):
    m = xr.shape[0]
    nt = m // n_seq // tt
    halo = (LRU_CONV - 1) * R
    row = lambda s, i: (s * nt + i, 0)
    seq = lambda s, i: (s, 0, 0)
    full2 = lambda s, i: (0, 0)
    return pl.pallas_call(
        functools.partial(_lru_kernel, R=R, tt=tt), name="rglru",
        grid=(n_seq, nt),
        in_specs=[pl.BlockSpec((tt, LRU_WIDTH), row), pl.BlockSpec((tt, LRU_WIDTH), row),
                  pl.BlockSpec((1, halo, LRU_WIDTH), seq), pl.BlockSpec((1, R, LRU_WIDTH), seq),
                  pl.BlockSpec((LRU_CONV, LRU_WIDTH), full2), pl.BlockSpec((1, LRU_WIDTH), full2),
                  pl.BlockSpec((2, LRU_BLOCKS, LRU_BLOCK_W, LRU_BLOCK_W), lambda s, i: (0, 0, 0, 0)),
                  pl.BlockSpec((2, LRU_WIDTH), full2), pl.BlockSpec((1, LRU_WIDTH), full2)],
        out_specs=[pl.BlockSpec((tt, LRU_WIDTH), row), pl.BlockSpec((1, R, LRU_WIDTH), seq),
                   pl.BlockSpec((1, halo, LRU_WIDTH), seq)],
        out_shape=[jax.ShapeDtypeStruct((m, LRU_WIDTH), F32), jax.ShapeDtypeStruct((n_seq, R, LRU_WIDTH), F32),
                   jax.ShapeDtypeStruct((n_seq, halo, LRU_WIDTH), F32)],
        scratch_shapes=[pltpu.VMEM((_halo_pad(halo) + halo + tt, LRU_WIDTH), F32),
                        pltpu.VMEM((tt, LRU_WIDTH), F32), pltpu.VMEM((tt, LRU_WIDTH), F32),
                        pltpu.VMEM((R, LRU_WIDTH), F32)],
        compiler_params=_params("arbitrary", "arbitrary"),
    )(xr, yr, buf0, h0, cw, cb, gw, gb, lsl)


def _conf_kernel(ua_ref, ub_ref, buf0_ref, cw_ref, cb_ref, out_ref, bufnew_ref, xbuf, *, R, tt, ct):
    halo = (CONF_CONV - 1) * R
    p0 = _halo_pad(halo)

    @pl.when(pl.program_id(2) == 0)
    def _():
        xbuf[p0:p0 + halo, :] = buf0_ref[0]

    xbuf[p0 + halo:p0 + halo + tt, :] = ua_ref[...] * _sigmoid(ub_ref[...])
    acc = jnp.broadcast_to(cb_ref[...], (tt, ct))
    for k in range(CONF_CONV):
        acc = acc + cw_ref[k:k + 1, :] * xbuf[p0 + k * R:p0 + k * R + tt, :]
    out_ref[...] = acc
    tail = xbuf[p0 + tt:p0 + tt + halo, :]
    bufnew_ref[0] = tail
    xbuf[p0:p0 + halo, :] = tail


def _conf_conv(u, buf0, cw, cb, *, n_seq, R, tt, ct):
    m = u.shape[0]
    nt = m // n_seq // tt
    nc = CONF_CH // ct
    halo = (CONF_CONV - 1) * R
    return pl.pallas_call(
        functools.partial(_conf_kernel, R=R, tt=tt, ct=ct), name="conf_conv",
        grid=(n_seq, nc, nt),
        in_specs=[pl.BlockSpec((tt, ct), lambda s, c, i: (s * nt + i, c)),
                  pl.BlockSpec((tt, ct), lambda s, c, i: (s * nt + i, nc + c)),
                  pl.BlockSpec((1, halo, ct), lambda s, c, i: (s, 0, c)),
                  pl.BlockSpec((CONF_CONV, ct), lambda s, c, i: (0, c)),
                  pl.BlockSpec((1, ct), lambda s, c, i: (0, c))],
        out_specs=[pl.BlockSpec((tt, ct), lambda s, c, i: (s * nt + i, c)),
                   pl.BlockSpec((1, halo, ct), lambda s, c, i: (s, 0, c))],
        out_shape=[jax.ShapeDtypeStruct((m, CONF_CH), F32), jax.ShapeDtypeStruct((n_seq, halo, CONF_CH), F32)],
        scratch_shapes=[pltpu.VMEM((_halo_pad(halo) + halo + tt, ct), F32)],
        compiler_params=_params("arbitrary", "arbitrary", "arbitrary"),
    )(u, u, buf0, cw, cb)


def _cmp_chunk_kernel(kv_ref, pe_ref, w1_ref, f_ref, s_ref, *, tc):
    xs = [kv_ref[pl.ds(l, tc, stride=CMP_STRIDE), :] for l in range(CMP_STRIDE)]
    for half, dst in ((0, f_ref), (1, s_ref)):
        xcat = jnp.concatenate(
            [(xs[l] + pe_ref[0, half * CMP_STRIDE + l:half * CMP_STRIDE + l + 1, :]).astype(BF16)
             for l in range(CMP_STRIDE)], axis=1)
        dst[...] = jnp.dot(xcat, w1_ref[0, half], preferred_element_type=F32)


def _cmp_chunks(kv_rows, pe, w1, page_chunks):
    nch = kv_rows.shape[0] // CMP_STRIDE
    n_pages = nch // page_chunks
    tc = page_chunks * max(d for d in range(1, 17) if n_pages % d == 0)
    out = jax.ShapeDtypeStruct((nch, KV_W), F32)
    return pl.pallas_call(
        functools.partial(_cmp_chunk_kernel, tc=tc), name="cmp_chunks",
        grid=(KV_W // HEAD_DIM, nch // tc),
        in_specs=[pl.BlockSpec((tc * CMP_STRIDE, HEAD_DIM), lambda c, i: (i, c)),
                  pl.BlockSpec((1, CMP_LEN, HEAD_DIM), lambda c, i: (c // N_KV, 0, 0)),
                  pl.BlockSpec((1, 2, CMP_STRIDE * HEAD_DIM, CMP_HID), lambda c, i: (c // N_KV, 0, 0, 0))],
        out_specs=[pl.BlockSpec((tc, HEAD_DIM), lambda c, i: (i, c)), pl.BlockSpec((tc, HEAD_DIM), lambda c, i: (i, c))],
        out_shape=[out, out],
        compiler_params=_params("arbitrary", "arbitrary"),
    )(kv_rows, pe, w1)


def _cmp_blocks_body(f, s, w2_ref, kc_ref, vc_ref):
    n = f.shape[0]
    hid = _gelu(f + pltpu.roll(s, n - 1, axis=0)).astype(BF16)
    for kind, dst in ((0, kc_ref), (1, vc_ref)):
        outs = [jnp.dot(hid[:, (kind * N_KV + g) * CMP_HID:(kind * N_KV + g + 1) * CMP_HID], w2_ref[kind],
                        preferred_element_type=F32) for g in range(N_KV)]
        dst[0] = jnp.concatenate(outs, axis=1).astype(BF16)


def _cmp_blocks_kernel(f_ref, s_ref, w2_ref, kc_ref, vc_ref):
    _cmp_blocks_body(f_ref[0], s_ref[0], w2_ref, kc_ref, vc_ref)


def _cmp_blocks(f, s, w2):
    b, n, _ = f.shape
    out = jax.ShapeDtypeStruct((b, n, N_KV * HEAD_DIM), BF16)
    blk = pl.BlockSpec((1, n, KV_W), lambda i: (i, 0, 0))
    oblk = pl.BlockSpec((1, n, N_KV * HEAD_DIM), lambda i: (i, 0, 0))
    return pl.pallas_call(
        _cmp_blocks_kernel, name="cmp_blocks",
        grid=(b,), in_specs=[blk, blk, pl.BlockSpec((2, CMP_HID, HEAD_DIM), lambda i: (0, 0, 0))],
        out_specs=[oblk, oblk], out_shape=[out, out],
        compiler_params=_params("arbitrary"),
    )(f, s, w2)


def _cmp_blocks_paged_kernel(pt_ref, *refs, n_pages):
    f_refs, s_refs = refs[:n_pages], refs[n_pages:2 * n_pages]
    w2_ref, kc_ref, vc_ref = refs[2 * n_pages:]
    f = jnp.concatenate([r[0] for r in f_refs], axis=0)
    s = jnp.concatenate([r[0] for r in s_refs], axis=0)
    _cmp_blocks_body(f, s, w2_ref, kc_ref, vc_ref)


def _cmp_blocks_paged(f, s, w2, page_table):
    db, n_pages = page_table.shape
    cpp = f.shape[1]
    n = n_pages * cpp
    out = jax.ShapeDtypeStruct((db, n, N_KV * HEAD_DIM), BF16)
    page = lambda p: pl.BlockSpec((1, cpp, KV_W), lambda b, pt: (pt[b, p], 0, 0))
    oblk = pl.BlockSpec((1, n, N_KV * HEAD_DIM), lambda b, pt: (b, 0, 0))
    grid_spec = pltpu.PrefetchScalarGridSpec(
        num_scalar_prefetch=1, grid=(db,),
        in_specs=[page(p) for p in range(n_pages)] * 2 + [pl.BlockSpec((2, CMP_HID, HEAD_DIM), lambda b, pt: (0, 0, 0))],
        out_specs=[oblk, oblk])
    return pl.pallas_call(
        functools.partial(_cmp_blocks_paged_kernel, n_pages=n_pages), name="cmp_blocks_paged",
        grid_spec=grid_spec, out_shape=[out, out],
        compiler_params=_params("arbitrary"),
    )(page_table, *([f] * n_pages), *([s] * n_pages), w2)


def _stack_heads(q, g):
    return jnp.concatenate([q[:, (g * HPG + h) * HEAD_DIM:(g * HPG + h + 1) * HEAD_DIM] for h in range(HPG)], axis=0)


def _qk(q4, k):
    return lax.dot_general(q4, k, (((1,), (1,)), ((), ())), preferred_element_type=F32)


def _masked_attn(q4, k, v, valid, nq):
    nk = k.shape[0]
    s = _qk(q4, k).reshape(HPG, nq, nk)
    sm = jnp.where(valid[None], s, NEG)
    m = jnp.max(sm, axis=-1, keepdims=True)
    e = jnp.where(valid[None], jnp.exp(sm - m), 0.0)
    den = jnp.sum(e, axis=-1, keepdims=True)
    p = e / jnp.where(den > 0.0, den, 1.0)
    o = jnp.dot(p.reshape(HPG * nq, nk).astype(BF16), v, preferred_element_type=F32)
    return o, p


def _online_step(q4, k, v, valid, carry, nq):
    m, l, acc = carry
    nk = k.shape[0]
    s = _qk(q4, k).reshape(HPG, nq, nk)
    sm = jnp.where(valid[None], s, NEG)
    m_new = jnp.maximum(m, jnp.max(sm, axis=-1, keepdims=True))
    alpha = jnp.exp(m - m_new)
    e = jnp.where(valid[None], jnp.exp(sm - m_new), 0.0)
    l = alpha * l + jnp.sum(e, axis=-1, keepdims=True)
    pv = jnp.dot(e.reshape(HPG * nq, nk).astype(BF16), v, preferred_element_type=F32)
    return m_new, l, alpha * acc + pv.reshape(HPG, nq, HEAD_DIM)


def _online_init(nq):
    return (jnp.full((HPG, nq, 1), NEG, F32), jnp.zeros((HPG, nq, 1), F32), jnp.zeros((HPG, nq, HEAD_DIM), F32))


def _online_finish(carry, nq):
    _, l, acc = carry
    return (acc / jnp.where(l > 0.0, l, 1.0)).reshape(HPG * nq, HEAD_DIM)


def _compressed_and_select(q4, kc, vc, msel, qpos, nq, n_cmp):
    ncp = kc.shape[0]
    ci = lax.broadcasted_iota(jnp.int32, (nq, ncp), 1)
    ok_c = (ci * CMP_STRIDE + (CMP_LEN - 1) <= qpos) & (ci < n_cmp)
    o_c, p = _masked_attn(q4, kc, vc, ok_c, nq)
    imp = jnp.sum(p, axis=0)
    imp_s = jnp.dot(imp, msel, preferred_element_type=F32, precision=lax.Precision.HIGHEST)
    blk = lax.broadcasted_iota(jnp.int32, (nq, LANES), 1)
    cur = qpos // SEL_BLOCK
    forced = (blk == 0) | (blk == cur) | (blk == cur - 1)
    score = jnp.where(forced, BIG, imp_s)
    score = jnp.where(blk * SEL_BLOCK <= qpos, score, -BIG)
    sel = jnp.zeros((nq, LANES), F32)
    for _ in range(N_SEL):
        top = jnp.max(score, axis=-1, keepdims=True)
        idx = jnp.min(jnp.where(score == top, blk, LANES), axis=-1, keepdims=True)
        pick = blk == idx
        sel = jnp.where(pick & (top > -0.5 * BIG), 1.0, sel)
        score = jnp.where(pick, -3.0 * BIG, score)
    return o_c, sel


def _gate_mix(gates, o_c, o_s, o_w, nq):
    outs = []
    for h in range(HPG):
        c0 = h * N_NSA_BRANCH
        rows = slice(h * nq, (h + 1) * nq)
        outs.append(gates[:, c0:c0 + 1] * o_c[rows] + gates[:, c0 + 1:c0 + 2] * o_s[rows]
                    + gates[:, c0 + 2:c0 + 3] * o_w[rows])
    return jnp.concatenate(outs, axis=1)


def _nsa_prompt_kernel(q_ref, kc_ref, vc_ref, ks_ref, vs_ref, kw_ref, vw_ref, gates_ref, msel_ref, o_ref,
                       *, seq, n_cmp, tk, wspan):
    j = pl.program_id(2)
    nq = Q_BLOCK
    start = j * Q_BLOCK
    qpos = start + lax.broadcasted_iota(jnp.int32, (nq, 1), 0)
    q4 = _stack_heads(q_ref[...], 0)
    o_c, sel = _compressed_and_select(q4, kc_ref[0], vc_ref[0], msel_ref[...], qpos, nq, n_cmp)
    sel_b = sel.astype(BF16)

    def body(kt, carry):
        k0 = pl.multiple_of(kt * tk, tk)
        kpos = k0 + lax.broadcasted_iota(jnp.int32, (nq, tk), 1)
        expand = (lax.broadcasted_iota(jnp.int32, (LANES, tk), 0)
                  == (k0 + lax.broadcasted_iota(jnp.int32, (LANES, tk), 1)) // SEL_BLOCK)
        picked = jnp.dot(sel_b, expand.astype(BF16), preferred_element_type=F32) > 0.5
        return _online_step(q4, ks_ref[pl.ds(k0, tk), :], vs_ref[pl.ds(k0, tk), :],
                            picked & (kpos <= qpos), carry, nq)

    n_tiles = (start + Q_BLOCK + tk - 1) // tk
    o_s = _online_finish(lax.fori_loop(0, n_tiles, body, _online_init(nq)), nq)

    w0 = pl.multiple_of(jnp.clip(start - WINDOW, 0, seq - wspan), Q_BLOCK)
    kwpos = w0 + lax.broadcasted_iota(jnp.int32, (nq, wspan), 1)
    ok_w = (kwpos <= qpos) & (kwpos > qpos - WINDOW)
    o_w, _ = _masked_attn(q4, kw_ref[pl.ds(w0, wspan), :], vw_ref[pl.ds(w0, wspan), :], ok_w, nq)
    o_ref[...] = _gate_mix(gates_ref[...], o_c, o_s, o_w, nq)


def _nsa_prompt(q, kc, vc, kvs_b, kvw_b, gates, msel, *, batch, seq):
    m = q.shape[0]
    nj = seq // Q_BLOCK
    ncp = kc.shape[1]
    tk = min(SEL_KEY_TILE, seq)
    wspan = min(WINDOW + Q_BLOCK, seq)
    gw = HPG * HEAD_DIM
    kcol = lambda c: pl.BlockSpec((seq, HEAD_DIM), lambda b, g, j: (b, c * N_KV + g))
    return pl.pallas_call(
        functools.partial(_nsa_prompt_kernel, seq=seq, n_cmp=seq // CMP_STRIDE - 1, tk=tk, wspan=wspan),
        name="nsa_prompt",
        grid=(batch, N_KV, nj),
        in_specs=[pl.BlockSpec((Q_BLOCK, gw), lambda b, g, j: (b * nj + j, g)),
                  pl.BlockSpec((1, ncp, HEAD_DIM), lambda b, g, j: (b, 0, g)),
                  pl.BlockSpec((1, ncp, HEAD_DIM), lambda b, g, j: (b, 0, g)),
                  kcol(0), kcol(1), kcol(0), kcol(1),
                  pl.BlockSpec((Q_BLOCK, LANES), lambda b, g, j: (b * nj + j, g)),
                  pl.BlockSpec((ncp, LANES), lambda b, g, j: (0, 0))],
        out_specs=pl.BlockSpec((Q_BLOCK, gw), lambda b, g, j: (b * nj + j, g)),
        out_shape=jax.ShapeDtypeStruct((m, N_HEADS * HEAD_DIM), F32),
        compiler_params=_params("arbitrary", "arbitrary", "arbitrary"),
    )(q, kc, vc, kvs_b, kvs_b, kvw_b, kvw_b, gates, msel)


def _nsa_sample_kernel(pt_ref, *refs, n_pages, past, wbuf):
    page_refs = refs[:n_pages]
    q_ref, kc_ref, vc_ref, ksn_ref, win_ref, kwn_ref, gates_ref, msel_ref, o_ref = refs[n_pages:]
    nq = SAMPLE_Q_PAD
    qpos = past + lax.broadcasted_iota(jnp.int32, (nq, 1), 0)
    lane = lax.broadcasted_iota(jnp.int32, (nq, PAGE_SIZE), 1)
    zeros = jnp.zeros((PAGE_SIZE - nq, HEAD_DIM), BF16)
    new_ok = (lane < nq) & (past + lane <= qpos)
    wpos = past - wbuf + lax.broadcasted_iota(jnp.int32, (nq, wbuf), 1)
    ok_w = (wpos <= qpos) & (wpos > qpos - WINDOW) & (wpos >= 0)
    outs = []
    for g in range(N_KV):
        kcols = slice(g * HEAD_DIM, (g + 1) * HEAD_DIM)
        vcols = slice((N_KV + g) * HEAD_DIM, (N_KV + g + 1) * HEAD_DIM)
        q4 = _stack_heads(q_ref[...], g)
        o_c, sel = _compressed_and_select(q4, kc_ref[0][:, kcols], vc_ref[0][:, kcols], msel_ref[...], qpos, nq,
                                          n_pages * PAGE_SIZE // CMP_STRIDE - 1)
        carry = _online_init(nq)
        for p in range(n_pages):
            first = p * PAGE_SIZE // SEL_BLOCK
            picked = jnp.where(lane < SEL_BLOCK, sel[:, first:first + 1], sel[:, first + 1:first + 2]) > 0.5
            carry = _online_step(q4, page_refs[p][0][:, kcols].astype(BF16), page_refs[p][0][:, vcols].astype(BF16),
                                 picked & (p * PAGE_SIZE + lane <= qpos), carry, nq)
        nb = past // SEL_BLOCK
        k_new = jnp.concatenate([ksn_ref[:, kcols].astype(BF16), zeros], axis=0)
        v_new = jnp.concatenate([ksn_ref[:, vcols].astype(BF16), zeros], axis=0)
        carry = _online_step(q4, k_new, v_new, (sel[:, nb:nb + 1] > 0.5) & new_ok, carry, nq)
        o_s = _online_finish(carry, nq)

        carry = _online_init(nq)
        carry = _online_step(q4, win_ref[0][:, kcols].astype(BF16), win_ref[0][:, vcols].astype(BF16), ok_w, carry, nq)
        kw_new = jnp.concatenate([kwn_ref[:, kcols].astype(BF16), zeros], axis=0)
        vw_new = jnp.concatenate([kwn_ref[:, vcols].astype(BF16), zeros], axis=0)
        carry = _online_step(q4, kw_new, vw_new, new_ok, carry, nq)
        o_w = _online_finish(carry, nq)
        outs.append(_gate_mix(gates_ref[:, g * LANES:(g + 1) * LANES], o_c, o_s, o_w, nq))
    o_ref[...] = jnp.concatenate(outs, axis=1)


def _nsa_sample(q8, kc, vc, slc_pool, kvs8, win, kvw8, gates8, msel, page_table, *, past):
    db, n_pages = page_table.shape
    nq = SAMPLE_Q_PAD
    wbuf = win.shape[1]
    n = kc.shape[1]
    page = lambda p: pl.BlockSpec((1, PAGE_SIZE, KV_W), lambda b, pt: (pt[b, p], 0, 0))
    rows = lambda w: pl.BlockSpec((nq, w), lambda b, pt: (b, 0))
    grid_spec = pltpu.PrefetchScalarGridSpec(
        num_scalar_prefetch=1, grid=(db,),
        in_specs=[page(p) for p in range(n_pages)] + [
            rows(N_HEADS * HEAD_DIM),
            pl.BlockSpec((1, n, N_KV * HEAD_DIM), lambda b, pt: (b, 0, 0)),
            pl.BlockSpec((1, n, N_KV * HEAD_DIM), lambda b, pt: (b, 0, 0)),
            rows(KV_W),
            pl.BlockSpec((1, wbuf, KV_W), lambda b, pt: (b, 0, 0)),
            rows(KV_W), rows(N_KV * LANES),
            pl.BlockSpec((n, LANES), lambda b, pt: (0, 0))],
        out_specs=rows(N_HEADS * HEAD_DIM))
    return pl.pallas_call(
        functools.partial(_nsa_sample_kernel, n_pages=n_pages, past=past, wbuf=wbuf), name="nsa_sample",
        grid_spec=grid_spec, out_shape=jax.ShapeDtypeStruct((db * nq, N_HEADS * HEAD_DIM), F32),
        compiler_params=_params("arbitrary"),
    )(page_table, *([slc_pool] * n_pages), q8, kc, vc, kvs8, win, kvw8, gates8, msel)


def _merge_kernel(a_ref, c_ref, o_ref, mg_ref, x_ref, wb_ref, wo_ref, lng_ref, lnb_ref, gpost_ref, y_ref):
    c = c_ref[...]
    mu = jnp.mean(c, axis=-1, keepdims=True)
    var = jnp.mean(jnp.square(c - mu), axis=-1, keepdims=True)
    n = (c - mu) * lax.rsqrt(var + EPS) * lng_ref[...] + lnb_ref[...]
    branches = (a_ref[...], n * _sigmoid(n), o_ref[...])
    merged = None
    for i, br in enumerate(branches):
        proj = jnp.dot(br.astype(BF16), wb_ref[i], preferred_element_type=F32)
        term = mg_ref[:, i * D_MODEL:(i + 1) * D_MODEL] * proj
        merged = term if merged is None else merged + term
    mix = jnp.dot(merged.astype(BF16), wo_ref[...], preferred_element_type=F32)
    y_ref[...] = x_ref[...] + _rms(mix, gpost_ref[...])


def _merge(a, c, o, mg, x, wb, wo, lng, lnb, gpost, tm):
    m = x.shape[0]
    row = lambda w: pl.BlockSpec((tm, w), lambda i: (i, 0))
    vec = pl.BlockSpec((1, D_MODEL), lambda i: (0, 0))
    return pl.pallas_call(
        _merge_kernel, name="merge",
        grid=(m // tm,),
        in_specs=[row(D_MODEL), row(D_MODEL), row(D_MODEL), row(N_BRANCH * D_MODEL), row(D_MODEL),
                  pl.BlockSpec((N_BRANCH, LRU_WIDTH, D_MODEL), lambda i: (0, 0, 0)),
                  pl.BlockSpec((D_MODEL, D_MODEL), lambda i: (0, 0)), vec, vec, vec],
        out_specs=row(D_MODEL), out_shape=jax.ShapeDtypeStruct((m, D_MODEL), F32),
        compiler_params=_params("arbitrary"),
    )(a, c, o, mg, x, wb, wo, lng, lnb, gpost)


def _ffn_in_kernel(x_ref, g_ref, w_ref, gate_ref, up_ref, h_ref):
    j = pl.program_id(1)
    n_gate = D_FF // COL_TILE

    @pl.when(j == 0)
    def _():
        h_ref[...] = _rms(x_ref[...], g_ref[...]).astype(BF16)

    z = jnp.dot(h_ref[...], w_ref[...], preferred_element_type=F32)

    @pl.when(j < n_gate)
    def _():
        gate_ref[...] = z

    @pl.when(j >= n_gate)
    def _():
        up_ref[...] = z


def _ffn_in(x, g, w, tm):
    m = x.shape[0]
    n_gate = D_FF // COL_TILE
    out = jax.ShapeDtypeStruct((m, D_FF), F32)
    return pl.pallas_call(
        _ffn_in_kernel, name="ffn_in",
        grid=(m // tm, 2 * n_gate),
        in_specs=[pl.BlockSpec((tm, D_MODEL), lambda i, j: (i, 0)), pl.BlockSpec((1, D_MODEL), lambda i, j: (0, 0)),
                  pl.BlockSpec((D_MODEL, COL_TILE), lambda i, j: (0, j))],
        out_specs=[pl.BlockSpec((tm, COL_TILE), lambda i, j: (i, jnp.minimum(j, n_gate - 1))),
                   pl.BlockSpec((tm, COL_TILE), lambda i, j: (i, jnp.maximum(j - n_gate, 0)))],
        out_shape=[out, out],
        scratch_shapes=[pltpu.VMEM((tm, D_MODEL), BF16)],
        compiler_params=_params("arbitrary", "arbitrary"),
    )(x, g, w)


def _ffn_out_kernel(gate_ref, up_ref, x_ref, buf0_ref, cw_ref, cb_ref, wo_ref, gpost_ref, y_ref, bufnew_ref, gbuf,
                    *, R, tt):
    halo = (FFN_CONV - 1) * R
    p0 = _halo_pad(halo)

    @pl.when(pl.program_id(1) == 0)
    def _():
        gbuf[p0:p0 + halo, :] = buf0_ref[0]

    gbuf[p0 + halo:p0 + halo + tt, :] = gate_ref[...]
    c = jnp.broadcast_to(cb_ref[...], (tt, D_FF))
    for k in range(FFN_CONV):
        c = c + cw_ref[k:k + 1, :] * gbuf[p0 + k * R:p0 + k * R + tt, :]
    act = (_gelu(c) * up_ref[...]).astype(BF16)
    f = jnp.dot(act, wo_ref[...], preferred_element_type=F32)
    y_ref[...] = x_ref[...] + _rms(f, gpost_ref[...])
    tail = gbuf[p0 + tt:p0 + tt + halo, :]
    bufnew_ref[0] = tail
    gbuf[p0:p0 + halo, :] = tail


def _ffn_out(gate, up, x, buf0, cw, cb, wo, gpost, *, n_seq, R, tt):
    m = x.shape[0]
    nt = m // n_seq // tt
    halo = (FFN_CONV - 1) * R
    row = lambda w: pl.BlockSpec((tt, w), lambda s, i: (s * nt + i, 0))
    full2 = lambda s, i: (0, 0)
    seq = pl.BlockSpec((1, halo, D_FF), lambda s, i: (s, 0, 0))
    return pl.pallas_call(
        functools.partial(_ffn_out_kernel, R=R, tt=tt), name="ffn_out",
        grid=(n_seq, nt),
        in_specs=[row(D_FF), row(D_FF), row(D_MODEL), seq,
                  pl.BlockSpec((FFN_CONV, D_FF), full2), pl.BlockSpec((1, D_FF), full2),
                  pl.BlockSpec((D_FF, D_MODEL), full2), pl.BlockSpec((1, D_MODEL), full2)],
        out_specs=[row(D_MODEL), seq],
        out_shape=[jax.ShapeDtypeStruct((m, D_MODEL), F32), jax.ShapeDtypeStruct((n_seq, halo, D_FF), F32)],
        scratch_shapes=[pltpu.VMEM((_halo_pad(halo) + halo + tt, D_FF), F32)],
        compiler_params=_params("arbitrary", "arbitrary"),
    )(gate, up, x, buf0, cw, cb, wo, gpost)


def _rope_tables(pos):
    half = HEAD_DIM // 2
    inv = ROPE_THETA ** (-jnp.arange(half, dtype=F32) / half)
    ang = pos.astype(F32)[:, None] * inv[None, :]
    cos, sin = jnp.cos(ang), jnp.sin(ang)
    return jnp.concatenate([cos, cos], axis=1), jnp.concatenate([-sin, sin], axis=1)


def _select_matrix(ncp):
    i = np.arange(ncp)[:, None]
    s = np.arange(LANES)[None, :]
    return jnp.asarray((i // CH_PER_SEL == s).astype(np.float32) + ((i + 1) // CH_PER_SEL == s).astype(np.float32))


def _layer_weights(l, norm_pre_mix, norm_post_mix, norm_pre_ffn, norm_post_ffn, w_in, lru_conv_w, lru_conv_b,
                   lru_gate_w, lru_gate_b, lru_lambda, conf_conv_w, conf_conv_b, conf_ln_g, conf_ln_b, cmp_pe,
                   cmp_w1, cmp_w2, w_branch, w_out, ffn_w_in, ffn_conv_w, ffn_conv_b, ffn_w_out):
    w = w_in[l]
    per_group = _NG_W // N_KV
    ng = jnp.concatenate([jnp.pad(w[:, _NG_OFF + g * per_group:_NG_OFF + (g + 1) * per_group],
                                  ((0, 0), (0, LANES - per_group))) for g in range(N_KV)], axis=1)
    ng = jnp.pad(ng, ((0, 0), (0, COL_TILE - N_KV * LANES)))
    w_perm = jnp.concatenate([w[:, :_NG_OFF], w[:, _NG_OFF + _NG_W:], ng], axis=1).astype(BF16)
    row = lambda v: v[l].reshape(1, -1)
    return dict(
        g_pre_mix=row(norm_pre_mix), g_post_mix=row(norm_post_mix), g_pre_ffn=row(norm_pre_ffn),
        g_post_ffn=row(norm_post_ffn), w_in=w_perm,
        lru_cw=lru_conv_w[l], lru_cb=row(lru_conv_b), lru_gw=lru_gate_w[l].astype(BF16),
        lru_gb=lru_gate_b[l].reshape(2, LRU_WIDTH), lru_lsl=jax.nn.log_sigmoid(lru_lambda[l].astype(F32)).reshape(1, -1),
        conf_cw=conf_conv_w[l], conf_cb=row(conf_conv_b), ln_g=row(conf_ln_g), ln_b=row(conf_ln_b),
        pe=cmp_pe[l], w1=cmp_w1[l].reshape(2, 2, CMP_STRIDE * HEAD_DIM, CMP_HID).astype(BF16),
        w2=cmp_w2[l].astype(BF16), wb=w_branch[l].astype(BF16), wo=w_out[l].astype(BF16),
        ffn_wi=ffn_w_in[l].astype(BF16), ffn_cw=ffn_conv_w[l], ffn_cb=row(ffn_conv_b),
        ffn_wo=ffn_w_out[l].astype(BF16))


def _layer(x, lw, cos, sin, state, attend, *, n_seq, R, tm, tt, conf_ct):
    lru_h0, lru_buf, conf_buf, ffn_buf = state
    xr, yr, u, q, kvc, kvs, kvw, kvs_b, kvw_b, mg, ng = _in_proj(x, lw["g_pre_mix"], lw["w_in"], cos, sin, tm)
    a_out, lru_h, lru_buf_new = _lru(xr, yr, lru_buf, lru_h0, lw["lru_cw"], lw["lru_cb"], lw["lru_gw"], lw["lru_gb"],
                                     lw["lru_lsl"], n_seq=n_seq, R=R, tt=tt)
    conv, conf_buf_new = _conf_conv(u, conf_buf, lw["conf_cw"], lw["conf_cb"], n_seq=n_seq, R=R,
                                    tt=x.shape[0] // n_seq if conf_ct < CONF_CH else tt, ct=conf_ct)
    o = attend(q, kvc, kvs, kvw, kvs_b, kvw_b, ng)
    x = _merge(a_out, conv, o, mg, x, lw["wb"], lw["wo"], lw["ln_g"], lw["ln_b"], lw["g_post_mix"], min(tm, 256))
    gate, up = _ffn_in(x, lw["g_pre_ffn"], lw["ffn_wi"], tm)
    x, ffn_buf_new = _ffn_out(gate, up, x, ffn_buf, lw["ffn_cw"], lw["ffn_cb"], lw["ffn_wo"], lw["g_post_ffn"],
                              n_seq=n_seq, R=R, tt=min(tt, 256) if R == 1 else R)
    return x, (kvc, kvs, kvw, lru_h, lru_buf_new, conf_buf_new, ffn_buf_new)


def kernel(x_prompt, x_sample, cache_cmp_kv, cache_slc_kv, cache_win_kv, state_lru_h, state_lru_conv, state_conf_conv, state_ffn_conv, page_table, norm_pre_mix, norm_post_mix, norm_pre_ffn, norm_post_ffn, w_in, lru_conv_w, lru_conv_b, lru_gate_w, lru_gate_b, lru_lambda, conf_conv_w, conf_conv_b, conf_ln_g, conf_ln_b, cmp_pe, cmp_w1, cmp_w2, w_branch, w_out, ffn_w_in, ffn_conv_w, ffn_conv_b, ffn_w_out):
    B, S, _ = x_prompt.shape
    DB, TS, _ = x_sample.shape
    depth = w_in.shape[0]
    n_pages = page_table.shape[1]
    past = n_pages * PAGE_SIZE
    n_pool = cache_cmp_kv.shape[1]
    wbuf = cache_win_kv.shape[2]
    assert S % SEL_KEY_TILE == 0 and TS <= SAMPLE_Q_PAD and DB % SUBLANES == 0 and wbuf % LANES == 0
    kv_row = (2, N_KV, HEAD_DIM)

    cos_p, sin_p = _rope_tables(jnp.arange(S))
    cos_s, sin_s = _rope_tables(past + jnp.repeat(jnp.arange(TS), DB))
    msel_p = _select_matrix(S // CMP_STRIDE)
    msel_s = _select_matrix(past // CMP_STRIDE)
    tm_p = 512
    tm_s = DB * TS
    to_tb = lambda a: jnp.swapaxes(a, 0, 1).reshape((TS * DB,) + a.shape[2:])
    to_bt = lambda a: jnp.swapaxes(a.reshape((TS, DB) + a.shape[1:]), 0, 1)
    pad_q = lambda a: jnp.pad(to_bt(a), ((0, 0), (0, SAMPLE_Q_PAD - TS), (0, 0))).reshape(DB * SAMPLE_Q_PAD, -1)
    state_in = lambda a: jnp.swapaxes(a, 0, 1).reshape(1, -1, a.shape[-1])
    state_out = lambda a, k: jnp.swapaxes(a.reshape(k, DB, a.shape[-1]), 0, 1)

    xp = x_prompt.reshape(B * S, D_MODEL)
    xs = to_tb(x_sample)
    st_p, st_s = [], []
    for l in range(depth):
        lw = _layer_weights(l, norm_pre_mix, norm_post_mix, norm_pre_ffn, norm_post_ffn, w_in, lru_conv_w,
                            lru_conv_b, lru_gate_w, lru_gate_b, lru_lambda, conf_conv_w, conf_conv_b, conf_ln_g,
                            conf_ln_b, cmp_pe, cmp_w1, cmp_w2, w_branch, w_out, ffn_w_in, ffn_conv_w, ffn_conv_b,
                            ffn_w_out)

        def attend_p(q, kvc, kvs, kvw, kvs_b, kvw_b, ng, lw=lw):
            f, s = _cmp_chunks(kvc, lw["pe"], lw["w1"], PAGE_SIZE // CMP_STRIDE)
            n = S // CMP_STRIDE
            kc, vc = _cmp_blocks(f.reshape(B, n, KV_W), s.reshape(B, n, KV_W), lw["w2"])
            return _nsa_prompt(q, kc, vc, kvs_b, kvw_b, ng, msel_p, batch=B, seq=S)

        zeros = lambda k, c: jnp.zeros((B, k, c), F32)
        xp, sp = _layer(xp, lw, cos_p, sin_p,
                        (zeros(1, LRU_WIDTH), zeros(LRU_CONV - 1, LRU_WIDTH), zeros(CONF_CONV - 1, CONF_CH),
                         zeros(FFN_CONV - 1, D_FF)),
                        attend_p, n_seq=B, R=1, tm=tm_p, tt=256, conf_ct=CONF_CH)

        def attend_s(q, kvc, kvs, kvw, kvs_b, kvw_b, ng, lw=lw, l=l):
            pool = cache_cmp_kv[l].reshape(n_pool * PAGE_SIZE, KV_W)
            cpp = PAGE_SIZE // CMP_STRIDE
            f, s = _cmp_chunks(pool, lw["pe"], lw["w1"], cpp)
            kc, vc = _cmp_blocks_paged(f.reshape(n_pool, cpp, KV_W), s.reshape(n_pool, cpp, KV_W), lw["w2"],
                                       page_table)
            o8 = _nsa_sample(pad_q(q), kc, vc, cache_slc_kv[l].reshape(n_pool, PAGE_SIZE, KV_W), pad_q(kvs),
                             cache_win_kv[l].reshape(DB, wbuf, KV_W), pad_q(kvw), pad_q(ng), msel_s, page_table,
                             past=past)
            return to_tb(o8.reshape(DB, SAMPLE_Q_PAD, -1)[:, :TS])

        xs, ss = _layer(xs, lw, cos_s, sin_s,
                        (state_lru_h[l][None], state_in(state_lru_conv[l]), state_in(state_conf_conv[l]),
                         state_in(state_ffn_conv[l])),
                        attend_s, n_seq=1, R=DB, tm=tm_s, tt=DB, conf_ct=256)
        st_p.append(sp)
        st_s.append(ss)

    def kv_p(i):
        return jnp.stack([s[i].reshape((B, S) + kv_row) for s in st_p])

    def kv_s(i):
        return jnp.stack([to_bt(s[i]).reshape((DB, TS) + kv_row) for s in st_s])

    win_p = jnp.stack([s[2].reshape((B, S) + kv_row)[:, S - min(WINDOW, S):] for s in st_p])
    win_s = jnp.stack([jnp.concatenate([cache_win_kv[l], to_bt(s[2]).reshape((DB, TS) + kv_row)], axis=1)
                       [:, wbuf + TS - min(WINDOW, past + TS):] for l, s in enumerate(st_s)])
    return (xp.reshape(B, S, D_MODEL), to_bt(xs),
            kv_p(0), kv_s(0), kv_p(1), kv_s(1), win_p, win_s,
            jnp.stack([s[3][:, 0] for s in st_p]), jnp.stack([s[3][0] for s in st_s]),
            jnp.stack([s[4] for s in st_p]), jnp.stack([state_out(s[4], LRU_CONV - 1) for s in st_s]),
            jnp.stack([s[5] for s in st_p]), jnp.stack([state_out(s[5], CONF_CONV - 1) for s in st_s]),
            jnp.stack([s[6] for s in st_p]), jnp.stack([state_out(s[6], FFN_CONV - 1) for s in st_s]))
```

```python
import functools

import jax
import jax.numpy as jnp
import numpy as np
from jax import lax
from jax.experimental import pallas as pl
from jax.experimental.pallas import tpu as pltpu

F32 = jnp.float32
BF16 = jnp.bfloat16

D_MODEL = 1024
PAGE_SIZE = 128
LRU_WIDTH = 1024
LRU_BLOCKS = 8
LRU_BLOCK_W = LRU_WIDTH // LRU_BLOCKS
LRU_CONV = 4
LRU_C = 8.0
CONF_CH = 1024
CONF_CONV = 31
N_HEADS = 8
N_KV = 2
HPG = N_HEADS // N_KV
HEAD_DIM = 128
CMP_LEN = 32
CMP_STRIDE = 16
CMP_HID = 128
SEL_BLOCK = 64
N_SEL = 16
CH_PER_SEL = SEL_BLOCK // CMP_STRIDE
WINDOW = 512
Q_BLOCK = 128
N_NSA_BRANCH = 3
ROPE_THETA = 10000.0
N_BRANCH = 3
D_FF = 3072
FFN_CONV = 3
EPS = 1e-6
NEG = -1e30
MASKED = 2.0 * NEG
BIG = 1e9

LANES = 128
SUBLANES = 8
KV_W = 2 * N_KV * HEAD_DIM
COL_TILE = 512
SEL_KEY_TILE = 512
SAMPLE_Q_PAD = 8
VMEM_LIMIT = 56 * 1024 * 1024

_T_XR, _T_YR, _T_U, _T_Q, _T_KVC, _T_KVS, _T_KVW, _T_MG, _T_NG = 0, 2, 4, 8, 10, 11, 12, 13, 19
_N_IN_TILES = 20
_NG_OFF = LRU_WIDTH * 2 + 2 * CONF_CH + N_HEADS * HEAD_DIM + 3 * KV_W
_NG_W = N_NSA_BRANCH * N_HEADS


def _params(*sem):
    return pltpu.CompilerParams(dimension_semantics=sem, vmem_limit_bytes=VMEM_LIMIT)


def _gelu(x):
    return 0.5 * x * (1.0 + jnp.tanh(0.7978845608028654 * (x + 0.044715 * (x * x * x))))


def _sigmoid(x):
    return 1.0 / (1.0 + jnp.exp(-x))


def _rms(x, g):
    return x * lax.rsqrt(jnp.mean(x * x, axis=-1, keepdims=True) + EPS) * g


def _rope_heads(z, cos, sin, nheads):
    outs = []
    for h in range(nheads):
        xh = z[:, h * HEAD_DIM:(h + 1) * HEAD_DIM]
        outs.append(xh * cos + pltpu.roll(xh, HEAD_DIM // 2, axis=1) * sin)
    return outs


def _in_proj_kernel(x_ref, g_ref, w_ref, cos_ref, sin_ref,
                    xr_ref, yr_ref, u_ref, q_ref, kvc_ref, kvs_ref, kvw_ref, kvsb_ref, kvwb_ref,
                    mg_ref, ng_ref, h_ref):
    j = pl.program_id(1)

    @pl.when(j == 0)
    def _():
        h_ref[...] = _rms(x_ref[...], g_ref[...]).astype(BF16)

    z = jnp.dot(h_ref[...], w_ref[...], preferred_element_type=F32)

    @pl.when(j < _T_YR)
    def _():
        xr_ref[...] = z

    @pl.when((j >= _T_YR) & (j < _T_U))
    def _():
        yr_ref[...] = z

    @pl.when((j >= _T_U) & (j < _T_Q))
    def _():
        u_ref[...] = z

    @pl.when((j >= _T_Q) & (j < _T_KVC))
    def _():
        heads = _rope_heads(z, cos_ref[...], sin_ref[...], COL_TILE // HEAD_DIM)
        q_ref[...] = (jnp.concatenate(heads, axis=1) * HEAD_DIM ** -0.5).astype(BF16)

    def kv_rows():
        heads = _rope_heads(z, cos_ref[...], sin_ref[...], N_KV)
        return jnp.concatenate(heads + [z[:, N_KV * HEAD_DIM:]], axis=1)

    @pl.when(j == _T_KVC)
    def _():
        kvc_ref[...] = kv_rows()

    @pl.when(j == _T_KVS)
    def _():
        kv = kv_rows()
        kvs_ref[...] = kv
        kvsb_ref[...] = kv.astype(BF16)

    @pl.when(j == _T_KVW)
    def _():
        kv = kv_rows()
        kvw_ref[...] = kv
        kvwb_ref[...] = kv.astype(BF16)

    @pl.when((j >= _T_MG) & (j < _T_NG))
    def _():
        mg_ref[...] = _sigmoid(z)

    @pl.when(j == _T_NG)
    def _():
        ng_ref[...] = _sigmoid(z[:, :N_KV * LANES])


def _in_proj(x, g, w, cos, sin, tm):
    m = x.shape[0]
    nper = cos.shape[0] // tm

    def cols(t0, n):
        return lambda i, j: (i, jnp.clip(j - t0, 0, n - 1))

    tile = lambda t0, n: pl.BlockSpec((tm, COL_TILE), cols(t0, n))
    out_shape = [jax.ShapeDtypeStruct((m, LRU_WIDTH), F32), jax.ShapeDtypeStruct((m, LRU_WIDTH), F32),
                 jax.ShapeDtypeStruct((m, 2 * CONF_CH), F32), jax.ShapeDtypeStruct((m, N_HEADS * HEAD_DIM), BF16),
                 jax.ShapeDtypeStruct((m, KV_W), F32), jax.ShapeDtypeStruct((m, KV_W), F32),
                 jax.ShapeDtypeStruct((m, KV_W), F32), jax.ShapeDtypeStruct((m, KV_W), BF16),
                 jax.ShapeDtypeStruct((m, KV_W), BF16), jax.ShapeDtypeStruct((m, N_BRANCH * D_MODEL), F32),
                 jax.ShapeDtypeStruct((m, N_KV * LANES), F32)]
    out_specs = [tile(_T_XR, 2), tile(_T_YR, 2), tile(_T_U, 4), tile(_T_Q, 2), tile(_T_KVC, 1), tile(_T_KVS, 1),
                 tile(_T_KVW, 1), tile(_T_KVS, 1), tile(_T_KVW, 1), tile(_T_MG, 6),
                 pl.BlockSpec((tm, N_KV * LANES), lambda i, j: (i, 0))]
    return pl.pallas_call(
        _in_proj_kernel, name="in_proj",
        grid=(m // tm, _N_IN_TILES),
        in_specs=[pl.BlockSpec((tm, D_MODEL), lambda i, j: (i, 0)),
                  pl.BlockSpec((1, D_MODEL), lambda i, j: (0, 0)),
                  pl.BlockSpec((D_MODEL, COL_TILE), lambda i, j: (0, j)),
                  pl.BlockSpec((tm, HEAD_DIM), lambda i, j: (i % nper, 0)),
                  pl.BlockSpec((tm, HEAD_DIM), lambda i, j: (i % nper, 0))],
        out_specs=out_specs, out_shape=out_shape,
        scratch_shapes=[pltpu.VMEM((tm, D_MODEL), BF16)],
        compiler_params=_params("arbitrary", "arbitrary"),
    )(x, g, w, cos, sin)


def _halo_pad(halo):
    return (-halo) % SUBLANES


def _lru_kernel(xr_ref, yr_ref, buf0_ref, h0_ref, cw_ref, cb_ref, gw_ref, gb_ref, lsl_ref,
                out_ref, hnew_ref, bufnew_ref, xbuf, a_s, b_s, hcar, *, R, tt):
    halo = (LRU_CONV - 1) * R
    p0 = _halo_pad(halo)

    @pl.when(pl.program_id(1) == 0)
    def _():
        xbuf[p0:p0 + halo, :] = buf0_ref[0]
        hcar[...] = h0_ref[0]

    xbuf[p0 + halo:p0 + halo + tt, :] = xr_ref[...]
    xc = jnp.broadcast_to(cb_ref[...], (tt, LRU_WIDTH))
    for k in range(LRU_CONV):
        xc = xc + cw_ref[k:k + 1, :] * xbuf[p0 + k * R:p0 + k * R + tt, :]
    gi, gr = [], []
    for n in range(LRU_BLOCKS):
        xb = xc[:, n * LRU_BLOCK_W:(n + 1) * LRU_BLOCK_W].astype(BF16)
        gi.append(jnp.dot(xb, gw_ref[0, n], preferred_element_type=F32))
        gr.append(jnp.dot(xb, gw_ref[1, n], preferred_element_type=F32))
    gi = _sigmoid(jnp.concatenate(gi, axis=1) + gb_ref[0:1, :])
    gr = _sigmoid(jnp.concatenate(gr, axis=1) + gb_ref[1:2, :])
    log_a = LRU_C * gr * lsl_ref[...]
    a_s[...] = jnp.exp(log_a)
    th = jnp.tanh(log_a)
    b_s[...] = xc * gi * jnp.sqrt(-2.0 * th / (1.0 - th))

    def step(s, h):
        rows = pl.ds(pl.multiple_of(s * R, R), R)
        h = a_s[rows, :] * h + b_s[rows, :]
        b_s[rows, :] = h
        return h

    h = lax.fori_loop(0, tt // R, step, hcar[...], unroll=8 if tt // R >= 8 else True)
    hcar[...] = h
    hnew_ref[0] = h
    out_ref[...] = b_s[...] * _gelu(yr_ref[...])
    tail = xbuf[p0 + tt:p0 + tt + halo, :]
    bufnew_ref[0] = tail
    xbuf[p0:p0 + halo, :] = tail


def _lru(xr, yr, buf0, h0, cw, cb, gw, gb, lsl, *, n_seq, R, tt):
    m = xr.shape[0]
    nt = m // n_seq // tt
    halo = (LRU_CONV - 1) * R
    row = lambda s, i: (s * nt + i, 0)
    seq = lambda s, i: (s, 0, 0)
    full2 = lambda s, i: (0, 0)
    return pl.pallas_call(
        functools.partial(_lru_kernel, R=R, tt=tt), name="rglru",
        grid=(n_seq, nt),
        in_specs=[pl.BlockSpec((tt, LRU_WIDTH), row), pl.BlockSpec((tt, LRU_WIDTH), row),
                  pl.BlockSpec((1, halo, LRU_WIDTH), seq), pl.BlockSpec((1, R, LRU_WIDTH), seq),
                  pl.BlockSpec((LRU_CONV, LRU_WIDTH), full2), pl.BlockSpec((1, LRU_WIDTH), full2),
                  pl.BlockSpec((2, LRU_BLOCKS, LRU_BLOCK_W, LRU_BLOCK_W), lambda s, i: (0, 0, 0, 0)),
                  pl.BlockSpec((2, LRU_WIDTH), full2), pl.BlockSpec((1, LRU_WIDTH), full2)],
        out_specs=[pl.BlockSpec((tt, LRU_WIDTH), row), pl.BlockSpec((1, R, LRU_WIDTH), seq),
                   pl.BlockSpec((1, halo, LRU_WIDTH), seq)],
        out_shape=[jax.ShapeDtypeStruct((m, LRU_WIDTH), F32), jax.ShapeDtypeStruct((n_seq, R, LRU_WIDTH), F32),
                   jax.ShapeDtypeStruct((n_seq, halo, LRU_WIDTH), F32)],
        scratch_shapes=[pltpu.VMEM((_halo_pad(halo) + halo + tt, LRU_WIDTH), F32),
                        pltpu.VMEM((tt, LRU_WIDTH), F32), pltpu.VMEM((tt, LRU_WIDTH), F32),
                        pltpu.VMEM((R, LRU_WIDTH), F32)],
        compiler_params=_params("arbitrary", "arbitrary"),
    )(xr, yr, buf0, h0, cw, cb, gw, gb, lsl)


def _conf_kernel(ua_ref, ub_ref, buf0_ref, cw_ref, cb_ref, out_ref, bufnew_ref, xbuf, *, R, tt, ct):
    halo = (CONF_CONV - 1) * R
    p0 = _halo_pad(halo)

    @pl.when(pl.program_id(2) == 0)
    def _():
        xbuf[p0:p0 + halo, :] = buf0_ref[0]

    xbuf[p0 + halo:p0 + halo + tt, :] = ua_ref[...] * _sigmoid(ub_ref[...])
    acc = jnp.broadcast_to(cb_ref[...], (tt, ct))
    for k in range(CONF_CONV):
        acc = acc + cw_ref[k:k + 1, :] * xbuf[p0 + k * R:p0 + k * R + tt, :]
    out_ref[...] = acc
    tail = xbuf[p0 + tt:p0 + tt + halo, :]
    bufnew_ref[0] = tail
    xbuf[p0:p0 + halo, :] = tail


def _conf_conv(u, buf0, cw, cb, *, n_seq, R, tt, ct):
    m = u.shape[0]
    nt = m // n_seq // tt
    nc = CONF_CH // ct
    halo = (CONF_CONV - 1) * R
    return pl.pallas_call(
        functools.partial(_conf_kernel, R=R, tt=tt, ct=ct), name="conf_conv",
        grid=(n_seq, nc, nt),
        in_specs=[pl.BlockSpec((tt, ct), lambda s, c, i: (s * nt + i, c)),
                  pl.BlockSpec((tt, ct), lambda s, c, i: (s * nt + i, nc + c)),
                  pl.BlockSpec((1, halo, ct), lambda s, c, i: (s, 0, c)),
                  pl.BlockSpec((CONF_CONV, ct), lambda s, c, i: (0, c)),
                  pl.BlockSpec((1, ct), lambda s, c, i: (0, c))],
        out_specs=[pl.BlockSpec((tt, ct), lambda s, c, i: (s * nt + i, c)),
                   pl.BlockSpec((1, halo, ct), lambda s, c, i: (s, 0, c))],
        out_shape=[jax.ShapeDtypeStruct((m, CONF_CH), F32), jax.ShapeDtypeStruct((n_seq, halo, CONF_CH), F32)],
        scratch_shapes=[pltpu.VMEM((_halo_pad(halo) + halo + tt, ct), F32)],
        compiler_params=_params("arbitrary", "arbitrary", "arbitrary"),
    )(u, u, buf0, cw, cb)


def _cmp_chunk_kernel(kv_ref, pe_ref, w1_ref, f_ref, s_ref, *, tc):
    xs = [kv_ref[pl.ds(l, tc, stride=CMP_STRIDE), :] for l in range(CMP_STRIDE)]
    for half, dst in ((0, f_ref), (1, s_ref)):
        xcat = jnp.concatenate(
            [(xs[l] + pe_ref[0, half * CMP_STRIDE + l:half * CMP_STRIDE + l + 1, :]).astype(BF16)
             for l in range(CMP_STRIDE)], axis=1)
        dst[...] = jnp.dot(xcat, w1_ref[0, half], preferred_element_type=F32)


def _cmp_chunks(kv_rows, pe, w1, page_chunks):
    nch = kv_rows.shape[0] // CMP_STRIDE
    n_pages = nch // page_chunks
    tc = page_chunks * max(d for d in range(1, 17) if n_pages % d == 0)
    out = jax.ShapeDtypeStruct((nch, KV_W), F32)
    return pl.pallas_call(
        functools.partial(_cmp_chunk_kernel, tc=tc), name="cmp_chunks",
        grid=(KV_W // HEAD_DIM, nch // tc),
        in_specs=[pl.BlockSpec((tc * CMP_STRIDE, HEAD_DIM), lambda c, i: (i, c)),
                  pl.BlockSpec((1, CMP_LEN, HEAD_DIM), lambda c, i: (c // N_KV, 0, 0)),
                  pl.BlockSpec((1, 2, CMP_STRIDE * HEAD_DIM, CMP_HID), lambda c, i: (c // N_KV, 0, 0, 0))],
        out_specs=[pl.BlockSpec((tc, HEAD_DIM), lambda c, i: (i, c)), pl.BlockSpec((tc, HEAD_DIM), lambda c, i: (i, c))],
        out_shape=[out, out],
        compiler_params=_params("arbitrary", "arbitrary"),
    )(kv_rows, pe, w1)


def _cmp_chunk_flat_kernel(kv_ref, pe_ref, w1_ref, f_ref, s_ref, *, tc):
    n_kv_rows = KV_W // HEAD_DIM
    for c in range(n_kv_rows):
        kind = c // N_KV
        xs = [kv_ref[pl.ds(l * n_kv_rows + c, tc, stride=CMP_STRIDE * n_kv_rows), :] for l in range(CMP_STRIDE)]
        for half, dst in ((0, f_ref), (1, s_ref)):
            xcat = jnp.concatenate(
                [(xs[l] + pe_ref[kind, half * CMP_STRIDE + l:half * CMP_STRIDE + l + 1, :]).astype(BF16)
                 for l in range(CMP_STRIDE)], axis=1)
            dst[:, c * HEAD_DIM:(c + 1) * HEAD_DIM] = jnp.dot(xcat, w1_ref[kind, half], preferred_element_type=F32)


def _cmp_chunks_flat(kv_flat, pe, w1, *, layer, n_pool):
    page_chunks = PAGE_SIZE // CMP_STRIDE
    nch = n_pool * page_chunks
    tc = page_chunks * max(d for d in range(1, 17) if n_pool % d == 0)
    steps = nch // tc
    rows = tc * CMP_STRIDE * (KV_W // HEAD_DIM)
    out = jax.ShapeDtypeStruct((nch, KV_W), F32)
    return pl.pallas_call(
        functools.partial(_cmp_chunk_flat_kernel, tc=tc), name="cmp_chunks_pool",
        grid=(steps,),
        in_specs=[pl.BlockSpec((rows, HEAD_DIM), lambda i: (layer * steps + i, 0)),
                  pl.BlockSpec((2, CMP_LEN, HEAD_DIM), lambda i: (0, 0, 0)),
                  pl.BlockSpec((2, 2, CMP_STRIDE * HEAD_DIM, CMP_HID), lambda i: (0, 0, 0, 0))],
        out_specs=[pl.BlockSpec((tc, KV_W), lambda i: (i, 0)), pl.BlockSpec((tc, KV_W), lambda i: (i, 0))],
        out_shape=[out, out],
        compiler_params=_params("arbitrary"),
    )(kv_flat, pe, w1)


def _cmp_blocks_body(f, s, w2_ref, kc_ref, vc_ref):
    n = f.shape[0]
    hid = _gelu(f + pltpu.roll(s, n - 1, axis=0)).astype(BF16)
    for kind, dst in ((0, kc_ref), (1, vc_ref)):
        outs = [jnp.dot(hid[:, (kind * N_KV + g) * CMP_HID:(kind * N_KV + g + 1) * CMP_HID], w2_ref[kind],
                        preferred_element_type=F32) for g in range(N_KV)]
        dst[0] = jnp.concatenate(outs, axis=1).astype(BF16)


def _cmp_blocks_kernel(f_ref, s_ref, w2_ref, kc_ref, vc_ref):
    _cmp_blocks_body(f_ref[0], s_ref[0], w2_ref, kc_ref, vc_ref)


def _cmp_blocks(f, s, w2):
    b, n, _ = f.shape
    out = jax.ShapeDtypeStruct((b, n, N_KV * HEAD_DIM), BF16)
    blk = pl.BlockSpec((1, n, KV_W), lambda i: (i, 0, 0))
    oblk = pl.BlockSpec((1, n, N_KV * HEAD_DIM), lambda i: (i, 0, 0))
    return pl.pallas_call(
        _cmp_blocks_kernel, name="cmp_blocks",
        grid=(b,), in_specs=[blk, blk, pl.BlockSpec((2, CMP_HID, HEAD_DIM), lambda i: (0, 0, 0))],
        out_specs=[oblk, oblk], out_shape=[out, out],
        compiler_params=_params("arbitrary"),
    )(f, s, w2)


def _cmp_blocks_paged_kernel(pt_ref, *refs, n_pages):
    f_refs, s_refs = refs[:n_pages], refs[n_pages:2 * n_pages]
    w2_ref, kc_ref, vc_ref = refs[2 * n_pages:]
    f = jnp.concatenate([r[0] for r in f_refs], axis=0)
    s = jnp.concatenate([r[0] for r in s_refs], axis=0)
    _cmp_blocks_body(f, s, w2_ref, kc_ref, vc_ref)


def _cmp_blocks_paged(f, s, w2, page_table):
    db, n_pages = page_table.shape
    cpp = f.shape[1]
    n = n_pages * cpp
    out = jax.ShapeDtypeStruct((db, n, N_KV * HEAD_DIM), BF16)
    page = lambda p: pl.BlockSpec((1, cpp, KV_W), lambda b, pt: (pt[b, p], 0, 0))
    oblk = pl.BlockSpec((1, n, N_KV * HEAD_DIM), lambda b, pt: (b, 0, 0))
    grid_spec = pltpu.PrefetchScalarGridSpec(
        num_scalar_prefetch=1, grid=(db,),
        in_specs=[page(p) for p in range(n_pages)] * 2 + [pl.BlockSpec((2, CMP_HID, HEAD_DIM), lambda b, pt: (0, 0, 0))],
        out_specs=[oblk, oblk])
    return pl.pallas_call(
        functools.partial(_cmp_blocks_paged_kernel, n_pages=n_pages), name="cmp_blocks_paged",
        grid_spec=grid_spec, out_shape=[out, out],
        compiler_params=_params("arbitrary"),
    )(page_table, *([f] * n_pages), *([s] * n_pages), w2)


def _stack_heads(q, g):
    return jnp.concatenate([q[:, (g * HPG + h) * HEAD_DIM:(g * HPG + h + 1) * HEAD_DIM] for h in range(HPG)], axis=0)


def _qk(q4, k):
    return lax.dot_general(q4, k, (((1,), (1,)), ((), ())), preferred_element_type=F32)


def _masked_attn(q4, k, v, valid, nq):
    nk = k.shape[0]
    s = _qk(q4, k).reshape(HPG, nq, nk)
    sm = jnp.where(valid[None], s, MASKED)
    m = jnp.maximum(jnp.max(sm, axis=-1, keepdims=True), NEG)
    e = jnp.exp(sm - m)
    den = jnp.sum(e, axis=-1, keepdims=True)
    p = e / jnp.where(den > 0.0, den, 1.0)
    o = jnp.dot(p.reshape(HPG * nq, nk).astype(BF16), v, preferred_element_type=F32)
    return o, p


def _online_step(q4, k, v, valid, carry, nq):
    m, l, acc = carry
    nk = k.shape[0]
    s = _qk(q4, k).reshape(HPG, nq, nk)
    sm = jnp.where(valid[None], s, MASKED)
    m_new = jnp.maximum(m, jnp.max(sm, axis=-1, keepdims=True))
    alpha = jnp.exp(m - m_new)
    e = jnp.exp(sm - m_new)
    l = alpha * l + jnp.sum(e, axis=-1, keepdims=True)
    pv = jnp.dot(e.reshape(HPG * nq, nk).astype(BF16), v, preferred_element_type=F32)
    return m_new, l, alpha * acc + pv.reshape(HPG, nq, HEAD_DIM)


def _online_init(nq):
    return (jnp.full((HPG, nq, 1), NEG, F32), jnp.zeros((HPG, nq, 1), F32), jnp.zeros((HPG, nq, HEAD_DIM), F32))


def _online_finish(carry, nq):
    _, l, acc = carry
    return (acc / jnp.where(l > 0.0, l, 1.0)).reshape(HPG * nq, HEAD_DIM)


def _compressed_attn(q4, kc, vc, qpos, nq, n_cmp):
    ncp = kc.shape[0]
    ci = lax.broadcasted_iota(jnp.int32, (nq, ncp), 1)
    ok_c = (ci * CMP_STRIDE + (CMP_LEN - 1) <= qpos) & (ci < n_cmp)
    o_c, p = _masked_attn(q4, kc, vc, ok_c, nq)
    return o_c, jnp.sum(p, axis=0)


def _select_blocks(imp, msel_t, qpos_row):
    imp_s = lax.dot_general(msel_t, imp, (((1,), (1,)), ((), ())), preferred_element_type=F32,
                            precision=lax.Precision.HIGHEST)
    blk = lax.broadcasted_iota(jnp.int32, (LANES, LANES), 0)
    cur = qpos_row // SEL_BLOCK
    forced = (blk == 0) | (blk == cur) | (blk == cur - 1)
    score = jnp.where(forced, BIG, imp_s)
    score = jnp.where(blk * SEL_BLOCK <= qpos_row, score, -BIG)
    sel = jnp.zeros((LANES, LANES), F32)
    for _ in range(N_SEL):
        top = jnp.max(score, axis=0, keepdims=True)
        idx = jnp.min(jnp.where(score == top, blk, LANES), axis=0, keepdims=True)
        pick = blk == idx
        sel = jnp.where(pick & (top > -0.5 * BIG), 1.0, sel)
        score = jnp.where(pick, -3.0 * BIG, score)
    return sel.T


def _gate_mix(gates, o_c, o_s, o_w, nq):
    outs = []
    for h in range(HPG):
        c0 = h * N_NSA_BRANCH
        rows = slice(h * nq, (h + 1) * nq)
        outs.append(gates[:, c0:c0 + 1] * o_c[rows] + gates[:, c0 + 1:c0 + 2] * o_s[rows]
                    + gates[:, c0 + 2:c0 + 3] * o_w[rows])
    return jnp.concatenate(outs, axis=1)


def _nsa_prompt_kernel(q_ref, kc_ref, vc_ref, ks_ref, vs_ref, kw_ref, vw_ref, gates_ref, msel_ref, expand_ref, o_ref,
                       *, seq, n_cmp, tk, wspan):
    j = pl.program_id(2)
    nq = Q_BLOCK
    start = j * Q_BLOCK
    qpos = start + lax.broadcasted_iota(jnp.int32, (nq, 1), 0)
    q4 = _stack_heads(q_ref[...], 0)
    o_c, imp = _compressed_attn(q4, kc_ref[0], vc_ref[0], qpos, nq, n_cmp)
    sel_b = _select_blocks(imp, msel_ref[...], start + lax.broadcasted_iota(jnp.int32, (1, nq), 1)).astype(BF16)

    def tile(kt, carry, diagonal):
        k0 = pl.multiple_of(kt * tk, tk)
        picked = jnp.dot(sel_b, expand_ref[:, pl.ds(k0, tk)], preferred_element_type=F32) > 0.5
        if diagonal:
            picked = picked & (k0 + lax.broadcasted_iota(jnp.int32, (nq, tk), 1) <= qpos)
        return _online_step(q4, ks_ref[pl.ds(k0, tk), :], vs_ref[pl.ds(k0, tk), :], picked, carry, nq)

    last = (start + Q_BLOCK - 1) // tk
    carry = lax.fori_loop(0, last, lambda kt, c: tile(kt, c, False), _online_init(nq))
    o_s = _online_finish(tile(last, carry, True), nq)

    w0 = pl.multiple_of(jnp.clip(start - WINDOW, 0, seq - wspan), Q_BLOCK)
    kwpos = w0 + lax.broadcasted_iota(jnp.int32, (nq, wspan), 1)
    ok_w = (kwpos <= qpos) & (kwpos > qpos - WINDOW)
    o_w, _ = _masked_attn(q4, kw_ref[pl.ds(w0, wspan), :], vw_ref[pl.ds(w0, wspan), :], ok_w, nq)
    o_ref[...] = _gate_mix(gates_ref[...], o_c, o_s, o_w, nq)


def _nsa_prompt(q, kc, vc, kvs_b, kvw_b, gates, msel, expand, *, batch, seq):
    m = q.shape[0]
    nj = seq // Q_BLOCK
    ncp = kc.shape[1]
    tk = min(SEL_KEY_TILE, seq)
    wspan = min(WINDOW + Q_BLOCK, seq)
    gw = HPG * HEAD_DIM
    kcol = lambda c: pl.BlockSpec((seq, HEAD_DIM), lambda b, g, j: (b, c * N_KV + g))
    return pl.pallas_call(
        functools.partial(_nsa_prompt_kernel, seq=seq, n_cmp=seq // CMP_STRIDE - 1, tk=tk, wspan=wspan),
        name="nsa_prompt",
        grid=(batch, N_KV, nj),
        in_specs=[pl.BlockSpec((Q_BLOCK, gw), lambda b, g, j: (b * nj + j, g)),
                  pl.BlockSpec((1, ncp, HEAD_DIM), lambda b, g, j: (b, 0, g)),
                  pl.BlockSpec((1, ncp, HEAD_DIM), lambda b, g, j: (b, 0, g)),
                  kcol(0), kcol(1), kcol(0), kcol(1),
                  pl.BlockSpec((Q_BLOCK, LANES), lambda b, g, j: (b * nj + j, g)),
                  pl.BlockSpec((LANES, ncp), lambda b, g, j: (0, 0)),
                  pl.BlockSpec((LANES, seq), lambda b, g, j: (0, 0))],
        out_specs=pl.BlockSpec((Q_BLOCK, gw), lambda b, g, j: (b * nj + j, g)),
        out_shape=jax.ShapeDtypeStruct((m, N_HEADS * HEAD_DIM), F32),
        compiler_params=_params("arbitrary", "arbitrary", "arbitrary"),
    )(q, kc, vc, kvs_b, kvs_b, kvw_b, kvw_b, gates, msel, expand)


def _nsa_sample_kernel(pt_ref, *refs, n_pages, past, wbuf, ts):
    page_refs = refs[:n_pages]
    (q_ref, kc_ref, vc_ref, ksn_ref, win_ref, kwn_ref, gates_ref, msel_ref, expand_ref,
     o_ref, wnew_ref) = refs[n_pages:]
    nq = SAMPLE_Q_PAD
    n_kv_rows = KV_W // HEAD_DIM
    n_cached = n_pages * PAGE_SIZE
    qpos = past + lax.broadcasted_iota(jnp.int32, (nq, 1), 0)
    zeros = jnp.zeros((PAGE_SIZE - nq, HEAD_DIM), BF16)
    kpos = lax.broadcasted_iota(jnp.int32, (nq, n_cached + PAGE_SIZE), 1)
    ok_s = (kpos <= qpos) & (kpos < past + nq)
    wpos = past - wbuf + lax.broadcasted_iota(jnp.int32, (nq, wbuf + PAGE_SIZE), 1)
    ok_w = (wpos <= qpos) & (wpos > qpos - WINDOW) & (wpos >= 0) & (wpos < past + nq)

    q4s, o_cs, imps = [], [], []
    for g in range(N_KV):
        gcols = slice(g * HEAD_DIM, (g + 1) * HEAD_DIM)
        q4s.append(_stack_heads(q_ref[...], g))
        o_c, imp = _compressed_attn(q4s[g], kc_ref[0][:, gcols], vc_ref[0][:, gcols], qpos, nq,
                                    n_cached // CMP_STRIDE - 1)
        o_cs.append(o_c)
        imps.append(imp)
    imp_all = jnp.concatenate(imps + [jnp.zeros((LANES - N_KV * nq, imps[0].shape[1]), F32)], axis=0)
    qpos_row = past + (lax.broadcasted_iota(jnp.int32, (1, LANES), 1) & (nq - 1))
    sel_all = _select_blocks(imp_all, msel_ref[...], qpos_row)

    def cached_rows(ref, n, c):
        return ref[pl.ds(c, n, stride=n_kv_rows), :].astype(BF16)

    def with_new(cached, new_ref, c):
        return jnp.concatenate(cached + [new_ref[:, c * HEAD_DIM:(c + 1) * HEAD_DIM].astype(BF16), zeros], axis=0)

    outs = []
    for g in range(N_KV):
        q4 = q4s[g]
        picked = jnp.dot(sel_all[g * nq:(g + 1) * nq].astype(BF16), expand_ref[...], preferred_element_type=F32) > 0.5
        k_all = with_new([cached_rows(r, PAGE_SIZE, g) for r in page_refs], ksn_ref, g)
        v_all = with_new([cached_rows(r, PAGE_SIZE, N_KV + g) for r in page_refs], ksn_ref, N_KV + g)
        o_s, _ = _masked_attn(q4, k_all, v_all, picked & ok_s, nq)
        kw_all = with_new([cached_rows(win_ref, wbuf, g)], kwn_ref, g)
        vw_all = with_new([cached_rows(win_ref, wbuf, N_KV + g)], kwn_ref, N_KV + g)
        o_w, _ = _masked_attn(q4, kw_all, vw_all, ok_w, nq)
        outs.append(_gate_mix(gates_ref[:, g * LANES:(g + 1) * LANES], o_cs[g], o_s, o_w, nq))
    o_ref[...] = jnp.concatenate(outs, axis=1)

    keep = (wbuf - ts) * n_kv_rows
    wnew_ref[0:keep, :] = win_ref[ts * n_kv_rows:wbuf * n_kv_rows, :]
    wnew_ref[keep:wbuf * n_kv_rows, :] = jnp.concatenate(
        [kwn_ref[t:t + 1, c * HEAD_DIM:(c + 1) * HEAD_DIM] for t in range(ts) for c in range(n_kv_rows)], axis=0)


def _nsa_sample(q8, kc, vc, slc_flat, kvs8, win_flat, kvw8, gates8, msel, expand, page_table, *,
                layer, n_pool, wbuf, past, ts):
    db, n_pages = page_table.shape
    nq = SAMPLE_Q_PAD
    n = kc.shape[1]
    n_kv_rows = KV_W // HEAD_DIM
    page = lambda p: pl.BlockSpec((PAGE_SIZE * n_kv_rows, HEAD_DIM), lambda b, pt: (layer * n_pool + pt[b, p], 0))
    rows = lambda w: pl.BlockSpec((nq, w), lambda b, pt: (b, 0))
    grid_spec = pltpu.PrefetchScalarGridSpec(
        num_scalar_prefetch=1, grid=(db,),
        in_specs=[page(p) for p in range(n_pages)] + [
            rows(N_HEADS * HEAD_DIM),
            pl.BlockSpec((1, n, N_KV * HEAD_DIM), lambda b, pt: (b, 0, 0)),
            pl.BlockSpec((1, n, N_KV * HEAD_DIM), lambda b, pt: (b, 0, 0)),
            rows(KV_W),
            pl.BlockSpec((wbuf * n_kv_rows, HEAD_DIM), lambda b, pt: (layer * db + b, 0)),
            rows(KV_W), rows(N_KV * LANES),
            pl.BlockSpec((LANES, n), lambda b, pt: (0, 0)),
            pl.BlockSpec((LANES, n_pages * PAGE_SIZE + PAGE_SIZE), lambda b, pt: (0, 0))],
        out_specs=[rows(N_HEADS * HEAD_DIM), pl.BlockSpec((wbuf * n_kv_rows, HEAD_DIM), lambda b, pt: (b, 0))])
    return pl.pallas_call(
        functools.partial(_nsa_sample_kernel, n_pages=n_pages, past=past, wbuf=wbuf, ts=ts), name="nsa_sample",
        grid_spec=grid_spec,
        out_shape=[jax.ShapeDtypeStruct((db * nq, N_HEADS * HEAD_DIM), F32),
                   jax.ShapeDtypeStruct((db * wbuf * n_kv_rows, HEAD_DIM), F32)],
        compiler_params=_params("arbitrary"),
    )(page_table, *([slc_flat] * n_pages), q8, kc, vc, kvs8, win_flat, kvw8, gates8, msel, expand)


def _merge_kernel(a_ref, c_ref, o_ref, mg_ref, x_ref, wb_ref, wo_ref, lng_ref, lnb_ref, gpost_ref, y_ref):
    c = c_ref[...]
    mu = jnp.mean(c, axis=-1, keepdims=True)
    var = jnp.mean(jnp.square(c - mu), axis=-1, keepdims=True)
    n = (c - mu) * lax.rsqrt(var + EPS) * lng_ref[...] + lnb_ref[...]
    branches = (a_ref[...], n * _sigmoid(n), o_ref[...])
    merged = None
    for i, br in enumerate(branches):
        proj = jnp.dot(br.astype(BF16), wb_ref[i], preferred_element_type=F32)
        term = mg_ref[:, i * D_MODEL:(i + 1) * D_MODEL] * proj
        merged = term if merged is None else merged + term
    mix = jnp.dot(merged.astype(BF16), wo_ref[...], preferred_element_type=F32)
    y_ref[...] = x_ref[...] + _rms(mix, gpost_ref[...])


def _merge(a, c, o, mg, x, wb, wo, lng, lnb, gpost, tm):
    m = x.shape[0]
    row = lambda w: pl.BlockSpec((tm, w), lambda i: (i, 0))
    vec = pl.BlockSpec((1, D_MODEL), lambda i: (0, 0))
    return pl.pallas_call(
        _merge_kernel, name="merge",
        grid=(m // tm,),
        in_specs=[row(D_MODEL), row(D_MODEL), row(D_MODEL), row(N_BRANCH * D_MODEL), row(D_MODEL),
                  pl.BlockSpec((N_BRANCH, LRU_WIDTH, D_MODEL), lambda i: (0, 0, 0)),
                  pl.BlockSpec((D_MODEL, D_MODEL), lambda i: (0, 0)), vec, vec, vec],
        out_specs=row(D_MODEL), out_shape=jax.ShapeDtypeStruct((m, D_MODEL), F32),
        compiler_params=_params("arbitrary"),
    )(a, c, o, mg, x, wb, wo, lng, lnb, gpost)


def _ffn_in_kernel(x_ref, g_ref, w_ref, gate_ref, up_ref, h_ref):
    j = pl.program_id(1)
    n_gate = D_FF // COL_TILE

    @pl.when(j == 0)
    def _():
        h_ref[...] = _rms(x_ref[...], g_ref[...]).astype(BF16)

    z = jnp.dot(h_ref[...], w_ref[...], preferred_element_type=F32)

    @pl.when(j < n_gate)
    def _():
        gate_ref[...] = z

    @pl.when(j >= n_gate)
    def _():
        up_ref[...] = z


def _ffn_in(x, g, w, tm):
    m = x.shape[0]
    n_gate = D_FF // COL_TILE
    out = jax.ShapeDtypeStruct((m, D_FF), F32)
    return pl.pallas_call(
        _ffn_in_kernel, name="ffn_in",
        grid=(m // tm, 2 * n_gate),
        in_specs=[pl.BlockSpec((tm, D_MODEL), lambda i, j: (i, 0)), pl.BlockSpec((1, D_MODEL), lambda i, j: (0, 0)),
                  pl.BlockSpec((D_MODEL, COL_TILE), lambda i, j: (0, j))],
        out_specs=[pl.BlockSpec((tm, COL_TILE), lambda i, j: (i, jnp.minimum(j, n_gate - 1))),
                   pl.BlockSpec((tm, COL_TILE), lambda i, j: (i, jnp.maximum(j - n_gate, 0)))],
        out_shape=[out, out],
        scratch_shapes=[pltpu.VMEM((tm, D_MODEL), BF16)],
        compiler_params=_params("arbitrary", "arbitrary"),
    )(x, g, w)


def _ffn_out_kernel(gate_ref, up_ref, x_ref, buf0_ref, cw_ref, cb_ref, wo_ref, gpost_ref, y_ref, bufnew_ref, gbuf,
                    *, R, tt):
    halo = (FFN_CONV - 1) * R
    p0 = _halo_pad(halo)

    @pl.when(pl.program_id(1) == 0)
    def _():
        gbuf[p0:p0 + halo, :] = buf0_ref[0]

    gbuf[p0 + halo:p0 + halo + tt, :] = gate_ref[...]
    c = jnp.broadcast_to(cb_ref[...], (tt, D_FF))
    for k in range(FFN_CONV):
        c = c + cw_ref[k:k + 1, :] * gbuf[p0 + k * R:p0 + k * R + tt, :]
    act = (_gelu(c) * up_ref[...]).astype(BF16)
    f = jnp.dot(act, wo_ref[...], preferred_element_type=F32)
    y_ref[...] = x_ref[...] + _rms(f, gpost_ref[...])
    tail = gbuf[p0 + tt:p0 + tt + halo, :]
    bufnew_ref[0] = tail
    gbuf[p0:p0 + halo, :] = tail


def _ffn_out(gate, up, x, buf0, cw, cb, wo, gpost, *, n_seq, R, tt):
    m = x.shape[0]
    nt = m // n_seq // tt
    halo = (FFN_CONV - 1) * R
    row = lambda w: pl.BlockSpec((tt, w), lambda s, i: (s * nt + i, 0))
    full2 = lambda s, i: (0, 0)
    seq = pl.BlockSpec((1, halo, D_FF), lambda s, i: (s, 0, 0))
    return pl.pallas_call(
        functools.partial(_ffn_out_kernel, R=R, tt=tt), name="ffn_out",
        grid=(n_seq, nt),
        in_specs=[row(D_FF), row(D_FF), row(D_MODEL), seq,
                  pl.BlockSpec((FFN_CONV, D_FF), full2), pl.BlockSpec((1, D_FF), full2),
                  pl.BlockSpec((D_FF, D_MODEL), full2), pl.BlockSpec((1, D_MODEL), full2)],
        out_specs=[row(D_MODEL), seq],
        out_shape=[jax.ShapeDtypeStruct((m, D_MODEL), F32), jax.ShapeDtypeStruct((n_seq, halo, D_FF), F32)],
        scratch_shapes=[pltpu.VMEM((_halo_pad(halo) + halo + tt, D_FF), F32)],
        compiler_params=_params("arbitrary", "arbitrary"),
    )(gate, up, x, buf0, cw, cb, wo, gpost)


def _rope_tables(pos):
    half = HEAD_DIM // 2
    inv = ROPE_THETA ** (-jnp.arange(half, dtype=F32) / half)
    ang = pos.astype(F32)[:, None] * inv[None, :]
    cos, sin = jnp.cos(ang), jnp.sin(ang)
    return jnp.concatenate([cos, cos], axis=1), jnp.concatenate([-sin, sin], axis=1)


def _select_matrix(ncp):
    i = np.arange(ncp)[None, :]
    s = np.arange(LANES)[:, None]
    return jnp.asarray((i // CH_PER_SEL == s).astype(np.float32) + ((i + 1) // CH_PER_SEL == s).astype(np.float32))


def _expand_matrix(n_keys):
    return jnp.asarray(np.arange(n_keys)[None, :] // SEL_BLOCK == np.arange(LANES)[:, None], dtype=BF16)


def _layer_weights(l, norm_pre_mix, norm_post_mix, norm_pre_ffn, norm_post_ffn, w_in, lru_conv_w, lru_conv_b,
                   lru_gate_w, lru_gate_b, lru_lambda, conf_conv_w, conf_conv_b, conf_ln_g, conf_ln_b, cmp_pe,
                   cmp_w1, cmp_w2, w_branch, w_out, ffn_w_in, ffn_conv_w, ffn_conv_b, ffn_w_out):
    w = w_in[l]
    per_group = _NG_W // N_KV
    ng = jnp.concatenate([jnp.pad(w[:, _NG_OFF + g * per_group:_NG_OFF + (g + 1) * per_group],
                                  ((0, 0), (0, LANES - per_group))) for g in range(N_KV)], axis=1)
    ng = jnp.pad(ng, ((0, 0), (0, COL_TILE - N_KV * LANES)))
    w_perm = jnp.concatenate([w[:, :_NG_OFF], w[:, _NG_OFF + _NG_W:], ng], axis=1).astype(BF16)
    row = lambda v: v[l].reshape(1, -1)
    return dict(
        g_pre_mix=row(norm_pre_mix), g_post_mix=row(norm_post_mix), g_pre_ffn=row(norm_pre_ffn),
        g_post_ffn=row(norm_post_ffn), w_in=w_perm,
        lru_cw=lru_conv_w[l], lru_cb=row(lru_conv_b), lru_gw=lru_gate_w[l].astype(BF16),
        lru_gb=lru_gate_b[l].reshape(2, LRU_WIDTH), lru_lsl=jax.nn.log_sigmoid(lru_lambda[l].astype(F32)).reshape(1, -1),
        conf_cw=conf_conv_w[l], conf_cb=row(conf_conv_b), ln_g=row(conf_ln_g), ln_b=row(conf_ln_b),
        pe=cmp_pe[l], w1=cmp_w1[l].reshape(2, 2, CMP_STRIDE * HEAD_DIM, CMP_HID).astype(BF16),
        w2=cmp_w2[l].astype(BF16), wb=w_branch[l].astype(BF16), wo=w_out[l].astype(BF16),
        ffn_wi=ffn_w_in[l].astype(BF16), ffn_cw=ffn_conv_w[l], ffn_cb=row(ffn_conv_b),
        ffn_wo=ffn_w_out[l].astype(BF16))


def _layer(x, lw, cos, sin, state, attend, *, n_seq, R, tm, tt, conf_ct):
    lru_h0, lru_buf, conf_buf, ffn_buf = state
    xr, yr, u, q, kvc, kvs, kvw, kvs_b, kvw_b, mg, ng = _in_proj(x, lw["g_pre_mix"], lw["w_in"], cos, sin, tm)
    a_out, lru_h, lru_buf_new = _lru(xr, yr, lru_buf, lru_h0, lw["lru_cw"], lw["lru_cb"], lw["lru_gw"], lw["lru_gb"],
                                     lw["lru_lsl"], n_seq=n_seq, R=R, tt=tt)
    conv, conf_buf_new = _conf_conv(u, conf_buf, lw["conf_cw"], lw["conf_cb"], n_seq=n_seq, R=R,
                                    tt=x.shape[0] // n_seq if conf_ct < CONF_CH else tt, ct=conf_ct)
    o, win_new = attend(q, kvc, kvs, kvw, kvs_b, kvw_b, ng)
    x = _merge(a_out, conv, o, mg, x, lw["wb"], lw["wo"], lw["ln_g"], lw["ln_b"], lw["g_post_mix"], min(tm, 256))
    gate, up = _ffn_in(x, lw["g_pre_ffn"], lw["ffn_wi"], tm)
    x, ffn_buf_new = _ffn_out(gate, up, x, ffn_buf, lw["ffn_cw"], lw["ffn_cb"], lw["ffn_wo"], lw["g_post_ffn"],
                              n_seq=n_seq, R=R, tt=min(tt, 256) if R == 1 else R)
    return x, (kvc, kvs, win_new, lru_h, lru_buf_new, conf_buf_new, ffn_buf_new)


def kernel(x_prompt, x_sample, cache_cmp_kv, cache_slc_kv, cache_win_kv, state_lru_h, state_lru_conv, state_conf_conv, state_ffn_conv, page_table, norm_pre_mix, norm_post_mix, norm_pre_ffn, norm_post_ffn, w_in, lru_conv_w, lru_conv_b, lru_gate_w, lru_gate_b, lru_lambda, conf_conv_w, conf_conv_b, conf_ln_g, conf_ln_b, cmp_pe, cmp_w1, cmp_w2, w_branch, w_out, ffn_w_in, ffn_conv_w, ffn_conv_b, ffn_w_out):
    B, S, _ = x_prompt.shape
    DB, TS, _ = x_sample.shape
    depth = w_in.shape[0]
    n_pages = page_table.shape[1]
    past = n_pages * PAGE_SIZE
    n_pool = cache_cmp_kv.shape[1]
    wbuf = cache_win_kv.shape[2]
    assert S % SEL_KEY_TILE == 0 and TS <= SAMPLE_Q_PAD and DB % SUBLANES == 0 and wbuf == WINDOW
    kv_row = (2, N_KV, HEAD_DIM)
    cmp_flat = cache_cmp_kv.reshape(-1, HEAD_DIM)
    slc_flat = cache_slc_kv.reshape(-1, HEAD_DIM)
    win_flat = cache_win_kv.reshape(-1, HEAD_DIM)

    cos_p, sin_p = _rope_tables(jnp.arange(S))
    cos_s, sin_s = _rope_tables(past + jnp.repeat(jnp.arange(TS), DB))
    msel_p = _select_matrix(S // CMP_STRIDE)
    msel_s = _select_matrix(past // CMP_STRIDE)
    expand_p = _expand_matrix(S)
    expand_s = _expand_matrix(past + PAGE_SIZE)
    tm_p = 512
    tm_s = DB * TS
    to_tb = lambda a: jnp.swapaxes(a, 0, 1).reshape((TS * DB,) + a.shape[2:])
    to_bt = lambda a: jnp.swapaxes(a.reshape((TS, DB) + a.shape[1:]), 0, 1)
    pad_q = lambda a: jnp.pad(to_bt(a), ((0, 0), (0, SAMPLE_Q_PAD - TS), (0, 0))).reshape(DB * SAMPLE_Q_PAD, -1)
    state_in = lambda a: jnp.swapaxes(a, 0, 1).reshape(1, -1, a.shape[-1])
    state_out = lambda a, k: jnp.swapaxes(a.reshape(k, DB, a.shape[-1]), 0, 1)

    xp = x_prompt.reshape(B * S, D_MODEL)
    xs = to_tb(x_sample)
    st_p, st_s = [], []
    for l in range(depth):
        lw = _layer_weights(l, norm_pre_mix, norm_post_mix, norm_pre_ffn, norm_post_ffn, w_in, lru_conv_w,
                            lru_conv_b, lru_gate_w, lru_gate_b, lru_lambda, conf_conv_w, conf_conv_b, conf_ln_g,
                            conf_ln_b, cmp_pe, cmp_w1, cmp_w2, w_branch, w_out, ffn_w_in, ffn_conv_w, ffn_conv_b,
                            ffn_w_out)

        def attend_p(q, kvc, kvs, kvw, kvs_b, kvw_b, ng, lw=lw):
            f, s = _cmp_chunks(kvc, lw["pe"], lw["w1"], PAGE_SIZE // CMP_STRIDE)
            n = S // CMP_STRIDE
            kc, vc = _cmp_blocks(f.reshape(B, n, KV_W), s.reshape(B, n, KV_W), lw["w2"])
            o = _nsa_prompt(q, kc, vc, kvs_b, kvw_b, ng, msel_p, expand_p, batch=B, seq=S)
            return o, kvw.reshape((B, S) + kv_row)[:, S - min(WINDOW, S):]

        zeros = lambda k, c: jnp.zeros((B, k, c), F32)
        xp, sp = _layer(xp, lw, cos_p, sin_p,
                        (zeros(1, LRU_WIDTH), zeros(LRU_CONV - 1, LRU_WIDTH), zeros(CONF_CONV - 1, CONF_CH),
                         zeros(FFN_CONV - 1, D_FF)),
                        attend_p, n_seq=B, R=1, tm=tm_p, tt=256, conf_ct=CONF_CH)

        def attend_s(q, kvc, kvs, kvw, kvs_b, kvw_b, ng, lw=lw, l=l):
            cpp = PAGE_SIZE // CMP_STRIDE
            f, s = _cmp_chunks_flat(cmp_flat, lw["pe"], lw["w1"], layer=l, n_pool=n_pool)
            kc, vc = _cmp_blocks_paged(f.reshape(n_pool, cpp, KV_W), s.reshape(n_pool, cpp, KV_W), lw["w2"],
                                       page_table)
            o8, win_new = _nsa_sample(pad_q(q), kc, vc, slc_flat, pad_q(kvs), win_flat, pad_q(kvw), pad_q(ng),
                                      msel_s, expand_s, page_table, layer=l, n_pool=n_pool, wbuf=wbuf, past=past,
                                      ts=TS)
            return to_tb(o8.reshape(DB, SAMPLE_Q_PAD, -1)[:, :TS]), win_new.reshape((DB, wbuf) + kv_row)

        xs, ss = _layer(xs, lw, cos_s, sin_s,
                        (state_lru_h[l][None], state_in(state_lru_conv[l]), state_in(state_conf_conv[l]),
                         state_in(state_ffn_conv[l])),
                        attend_s, n_seq=1, R=DB, tm=tm_s, tt=DB, conf_ct=256)
        st_p.append(sp)
        st_s.append(ss)

    def kv_p(i):
        return jnp.stack([s[i].reshape((B, S) + kv_row) for s in st_p])

    def kv_s(i):
        return jnp.stack([to_bt(s[i]).reshape((DB, TS) + kv_row) for s in st_s])

    return (xp.reshape(B, S, D_MODEL), to_bt(xs),
            kv_p(0), kv_s(0), kv_p(1), kv_s(1), jnp.stack([s[2] for s in st_p]), jnp.stack([s[2] for s in st_s]),
            jnp.stack([s[3][:, 0] for s in st_p]), jnp.stack([s[3][0] for s in st_s]),
            jnp.stack([s[4] for s in st_p]), jnp.stack([state_out(s[4], LRU_CONV - 1) for s in st_s]),
            jnp.stack([s[5] for s in st_p]), jnp.stack([state_out(s[5], CONF_CONV - 1) for s in st_s]),
            jnp.stack([s[6] for s in st_p]), jnp.stack([state_out(s[6], FFN_CONV - 1) for s in st_s]))
```

```python
import functools

import jax
import jax.numpy as jnp
import numpy as np
from jax import lax
from jax.experimental import pallas as pl
from jax.experimental.pallas import tpu as pltpu

F32 = jnp.float32
BF16 = jnp.bfloat16

D_MODEL = 1024
PAGE_SIZE = 128
LRU_WIDTH = 1024
LRU_BLOCKS = 8
LRU_BLOCK_W = LRU_WIDTH // LRU_BLOCKS
LRU_CONV = 4
LRU_C = 8.0
CONF_CH = 1024
CONF_CONV = 31
N_HEADS = 8
N_KV = 2
HPG = N_HEADS // N_KV
HEAD_DIM = 128
CMP_LEN = 32
CMP_STRIDE = 16
CMP_HID = 128
SEL_BLOCK = 64
N_SEL = 16
N_FORCED = 3
CH_PER_SEL = SEL_BLOCK // CMP_STRIDE
WINDOW = 512
Q_BLOCK = 128
N_NSA_BRANCH = 3
ROPE_THETA = 10000.0
N_BRANCH = 3
D_FF = 3072
FFN_CONV = 3
EPS = 1e-6
NEG = -1e30
MASKED = 2.0 * NEG
BIG = 1e9

LANES = 128
SUBLANES = 8
KV_W = 2 * N_KV * HEAD_DIM
COL_TILE = 512
FFN_COL_TILE = 1024
SEL_KEY_TILE = 512
SAMPLE_Q_PAD = 8
CONV_ACC_VREGS = 32
VMEM_LIMIT = 56 * 1024 * 1024

_T_XR, _T_YR, _T_U, _T_Q, _T_KVC, _T_KVS, _T_KVW, _T_MG, _T_NG = 0, 2, 4, 8, 10, 11, 12, 13, 19
_N_IN_TILES = 20
_NG_OFF = LRU_WIDTH * 2 + 2 * CONF_CH + N_HEADS * HEAD_DIM + 3 * KV_W
_NG_W = N_NSA_BRANCH * N_HEADS


def _params(*sem):
    return pltpu.CompilerParams(dimension_semantics=sem, vmem_limit_bytes=VMEM_LIMIT)


def _gelu(x):
    return 0.5 * x * (1.0 + jnp.tanh(0.7978845608028654 * (x + 0.044715 * (x * x * x))))


def _sigmoid(x):
    return 1.0 / (1.0 + jnp.exp(-x))


def _rms(x, g):
    return x * lax.rsqrt(jnp.mean(x * x, axis=-1, keepdims=True) + EPS) * g


def _rope_heads(z, cos, sin, nheads):
    outs = []
    for h in range(nheads):
        xh = z[:, h * HEAD_DIM:(h + 1) * HEAD_DIM]
        outs.append(xh * cos + pltpu.roll(xh, HEAD_DIM // 2, axis=1) * sin)
    return outs


def _in_proj_kernel(x_ref, g_ref, w_ref, cos_ref, sin_ref,
                    xr_ref, yr_ref, u_ref, q_ref, kvc_ref, kvs_ref, kvw_ref, kvsb_ref, kvwb_ref,
                    mg_ref, ng_ref, h_ref):
    j = pl.program_id(1)

    @pl.when(j == 0)
    def _():
        h_ref[...] = _rms(x_ref[...], g_ref[...]).astype(BF16)

    z = jnp.dot(h_ref[...], w_ref[...], preferred_element_type=F32)

    @pl.when(j < _T_YR)
    def _():
        xr_ref[...] = z

    @pl.when((j >= _T_YR) & (j < _T_U))
    def _():
        yr_ref[...] = z

    @pl.when((j >= _T_U) & (j < _T_Q))
    def _():
        u_ref[...] = z

    @pl.when((j >= _T_Q) & (j < _T_KVC))
    def _():
        heads = _rope_heads(z, cos_ref[...], sin_ref[...], COL_TILE // HEAD_DIM)
        q_ref[...] = (jnp.concatenate(heads, axis=1) * HEAD_DIM ** -0.5).astype(BF16)

    def kv_rows():
        heads = _rope_heads(z, cos_ref[...], sin_ref[...], N_KV)
        return jnp.concatenate(heads + [z[:, N_KV * HEAD_DIM:]], axis=1)

    @pl.when(j == _T_KVC)
    def _():
        kvc_ref[...] = kv_rows()

    @pl.when(j == _T_KVS)
    def _():
        kv = kv_rows()
        kvs_ref[...] = kv
        kvsb_ref[...] = kv.astype(BF16)

    @pl.when(j == _T_KVW)
    def _():
        kv = kv_rows()
        kvw_ref[...] = kv
        kvwb_ref[...] = kv.astype(BF16)

    @pl.when((j >= _T_MG) & (j < _T_NG))
    def _():
        mg_ref[...] = _sigmoid(z)

    @pl.when(j == _T_NG)
    def _():
        ng_ref[...] = _sigmoid(z[:, :N_KV * LANES])


def _in_proj(x, g, w, cos, sin, tm):
    m = x.shape[0]
    nper = cos.shape[0] // tm

    def cols(t0, n):
        return lambda i, j: (i, jnp.clip(j - t0, 0, n - 1))

    tile = lambda t0, n: pl.BlockSpec((tm, COL_TILE), cols(t0, n))
    out_shape = [jax.ShapeDtypeStruct((m, LRU_WIDTH), F32), jax.ShapeDtypeStruct((m, LRU_WIDTH), F32),
                 jax.ShapeDtypeStruct((m, 2 * CONF_CH), F32), jax.ShapeDtypeStruct((m, N_HEADS * HEAD_DIM), BF16),
                 jax.ShapeDtypeStruct((m, KV_W), F32), jax.ShapeDtypeStruct((m, KV_W), F32),
                 jax.ShapeDtypeStruct((m, KV_W), F32), jax.ShapeDtypeStruct((m, KV_W), BF16),
                 jax.ShapeDtypeStruct((m, KV_W), BF16), jax.ShapeDtypeStruct((m, N_BRANCH * D_MODEL), F32),
                 jax.ShapeDtypeStruct((m, N_KV * LANES), F32)]
    out_specs = [tile(_T_XR, 2), tile(_T_YR, 2), tile(_T_U, 4), tile(_T_Q, 2), tile(_T_KVC, 1), tile(_T_KVS, 1),
                 tile(_T_KVW, 1), tile(_T_KVS, 1), tile(_T_KVW, 1), tile(_T_MG, 6),
                 pl.BlockSpec((tm, N_KV * LANES), lambda i, j: (i, 0))]
    return pl.pallas_call(
        _in_proj_kernel, name="in_proj",
        grid=(m // tm, _N_IN_TILES),
        in_specs=[pl.BlockSpec((tm, D_MODEL), lambda i, j: (i, 0)),
                  pl.BlockSpec((1, D_MODEL), lambda i, j: (0, 0)),
                  pl.BlockSpec((D_MODEL, COL_TILE), lambda i, j: (0, j)),
                  pl.BlockSpec((tm, HEAD_DIM), lambda i, j: (i % nper, 0)),
                  pl.BlockSpec((tm, HEAD_DIM), lambda i, j: (i % nper, 0))],
        out_specs=out_specs, out_shape=out_shape,
        scratch_shapes=[pltpu.VMEM((tm, D_MODEL), BF16)],
        compiler_params=_params("arbitrary", "arbitrary"),
    )(x, g, w, cos, sin)


def _halo_pad(halo):
    return (-halo) % SUBLANES


def _lru_kernel(xr_ref, yr_ref, buf0_ref, h0_ref, cw_ref, cb_ref, gw_ref, gb_ref, lsl_ref,
                out_ref, hnew_ref, bufnew_ref, xbuf, a_s, b_s, hcar, *, R, tt):
    halo = (LRU_CONV - 1) * R
    p0 = _halo_pad(halo)

    @pl.when(pl.program_id(1) == 0)
    def _():
        xbuf[p0:p0 + halo, :] = buf0_ref[0]
        hcar[...] = h0_ref[0]

    xbuf[p0 + halo:p0 + halo + tt, :] = xr_ref[...]
    xc = jnp.broadcast_to(cb_ref[...], (tt, LRU_WIDTH))
    for k in range(LRU_CONV):
        xc = xc + cw_ref[k:k + 1, :] * xbuf[p0 + k * R:p0 + k * R + tt, :]
    gi, gr = [], []
    for n in range(LRU_BLOCKS):
        xb = xc[:, n * LRU_BLOCK_W:(n + 1) * LRU_BLOCK_W].astype(BF16)
        gi.append(jnp.dot(xb, gw_ref[0, n], preferred_element_type=F32))
        gr.append(jnp.dot(xb, gw_ref[1, n], preferred_element_type=F32))
    gi = _sigmoid(jnp.concatenate(gi, axis=1) + gb_ref[0:1, :])
    gr = _sigmoid(jnp.concatenate(gr, axis=1) + gb_ref[1:2, :])
    log_a = LRU_C * gr * lsl_ref[...]
    a_s[...] = jnp.exp(log_a)
    th = jnp.tanh(log_a)
    b_s[...] = xc * gi * jnp.sqrt(-2.0 * th / (1.0 - th))

    def step(s, h):
        rows = pl.ds(pl.multiple_of(s * R, R), R)
        h = a_s[rows, :] * h + b_s[rows, :]
        b_s[rows, :] = h
        return h

    h = lax.fori_loop(0, tt // R, step, hcar[...], unroll=8 if tt // R >= 8 else True)
    hcar[...] = h
    hnew_ref[0] = h
    out_ref[...] = b_s[...] * _gelu(yr_ref[...])
    tail = xbuf[p0 + tt:p0 + tt + halo, :]
    bufnew_ref[0] = tail
    xbuf[p0:p0 + halo, :] = tail


def _lru(xr, yr, buf0, h0, cw, cb, gw, gb, lsl, *, n_seq, R, tt):
    m = xr.shape[0]
    nt = m // n_seq // tt
    halo = (LRU_CONV - 1) * R
    row = lambda s, i: (s * nt + i, 0)
    seq = lambda s, i: (s, 0, 0)
    full2 = lambda s, i: (0, 0)
    return pl.pallas_call(
        functools.partial(_lru_kernel, R=R, tt=tt), name="rglru",
        grid=(n_seq, nt),
        in_specs=[pl.BlockSpec((tt, LRU_WIDTH), row), pl.BlockSpec((tt, LRU_WIDTH), row),
                  pl.BlockSpec((1, halo, LRU_WIDTH), seq), pl.BlockSpec((1, R, LRU_WIDTH), seq),
                  pl.BlockSpec((LRU_CONV, LRU_WIDTH), full2), pl.BlockSpec((1, LRU_WIDTH), full2),
                  pl.BlockSpec((2, LRU_BLOCKS, LRU_BLOCK_W, LRU_BLOCK_W), lambda s, i: (0, 0, 0, 0)),
                  pl.BlockSpec((2, LRU_WIDTH), full2), pl.BlockSpec((1, LRU_WIDTH), full2)],
        out_specs=[pl.BlockSpec((tt, LRU_WIDTH), row), pl.BlockSpec((1, R, LRU_WIDTH), seq),
                   pl.BlockSpec((1, halo, LRU_WIDTH), seq)],
        out_shape=[jax.ShapeDtypeStruct((m, LRU_WIDTH), F32), jax.ShapeDtypeStruct((n_seq, R, LRU_WIDTH), F32),
                   jax.ShapeDtypeStruct((n_seq, halo, LRU_WIDTH), F32)],
        scratch_shapes=[pltpu.VMEM((_halo_pad(halo) + halo + tt, LRU_WIDTH), F32),
                        pltpu.VMEM((tt, LRU_WIDTH), F32), pltpu.VMEM((tt, LRU_WIDTH), F32),
                        pltpu.VMEM((R, LRU_WIDTH), F32)],
        compiler_params=_params("arbitrary", "arbitrary"),
    )(xr, yr, buf0, h0, cw, cb, gw, gb, lsl)


def _conf_kernel(ua_ref, ub_ref, buf0_ref, cw_ref, cb_ref, out_ref, bufnew_ref, xbuf, *phase_buf, R, tt, ct, rb):
    halo = (CONF_CONV - 1) * R
    p0 = _halo_pad(halo)

    @pl.when(pl.program_id(2) == 0)
    def _():
        xbuf[p0:p0 + halo, :] = buf0_ref[0]

    xbuf[p0 + halo:p0 + halo + tt, :] = ua_ref[...] * _sigmoid(ub_ref[...])
    offs = [p0 + k * R for k in range(CONF_CONV)]
    if phase_buf:
        sbuf, = phase_buf
        for b in range(SUBLANES):
            mine = [o for o in offs if o % SUBLANES == b]
            if mine:
                n = max(mine) - b + tt
                sbuf[b, 0:n, :] = xbuf[b:b + n, :]
        tap = lambda o, r0: sbuf[o % SUBLANES, pl.ds(r0 + (o - o % SUBLANES), rb), :]
    else:
        tap = lambda o, r0: xbuf[pl.ds(r0 + o, rb), :]

    def row_block(i, carry):
        r0 = pl.multiple_of(i * rb, rb)
        acc = jnp.broadcast_to(cb_ref[...], (rb, ct))
        for k, o in enumerate(offs):
            acc = acc + cw_ref[k:k + 1, :] * tap(o, r0)
        out_ref[pl.ds(r0, rb), :] = acc
        return carry

    lax.fori_loop(0, tt // rb, row_block, 0)
    tail = xbuf[p0 + tt:p0 + tt + halo, :]
    bufnew_ref[0] = tail
    xbuf[p0:p0 + halo, :] = tail


def _conf_conv(u, buf0, cw, cb, *, n_seq, R, tt, ct):
    m = u.shape[0]
    nt = m // n_seq // tt
    nc = CONF_CH // ct
    halo = (CONF_CONV - 1) * R
    rows = _halo_pad(halo) + halo + tt
    rb = min(tt, CONV_ACC_VREGS * SUBLANES * LANES // ct)
    scratch = [pltpu.VMEM((rows, ct), F32)]
    if R % SUBLANES:
        scratch.append(pltpu.VMEM((SUBLANES, rows, ct), F32))
    return pl.pallas_call(
        functools.partial(_conf_kernel, R=R, tt=tt, ct=ct, rb=rb), name="conf_conv",
        grid=(n_seq, nc, nt),
        in_specs=[pl.BlockSpec((tt, ct), lambda s, c, i: (s * nt + i, c)),
                  pl.BlockSpec((tt, ct), lambda s, c, i: (s * nt + i, nc + c)),
                  pl.BlockSpec((1, halo, ct), lambda s, c, i: (s, 0, c)),
                  pl.BlockSpec((CONF_CONV, ct), lambda s, c, i: (0, c)),
                  pl.BlockSpec((1, ct), lambda s, c, i: (0, c))],
        out_specs=[pl.BlockSpec((tt, ct), lambda s, c, i: (s * nt + i, c)),
                   pl.BlockSpec((1, halo, ct), lambda s, c, i: (s, 0, c))],
        out_shape=[jax.ShapeDtypeStruct((m, CONF_CH), F32), jax.ShapeDtypeStruct((n_seq, halo, CONF_CH), F32)],
        scratch_shapes=scratch,
        compiler_params=_params("arbitrary", "arbitrary", "arbitrary"),
    )(u, u, buf0, cw, cb)


def _cmp_chunk_kernel(kv_ref, pe_ref, w1_ref, f_ref, s_ref, *, tc):
    xs = [kv_ref[pl.ds(l, tc, stride=CMP_STRIDE), :] for l in range(CMP_STRIDE)]
    for half, dst in ((0, f_ref), (1, s_ref)):
        xcat = jnp.concatenate(
            [(xs[l] + pe_ref[0, half * CMP_STRIDE + l:half * CMP_STRIDE + l + 1, :]).astype(BF16)
             for l in range(CMP_STRIDE)], axis=1)
        dst[...] = jnp.dot(xcat, w1_ref[0, half], preferred_element_type=F32)


def _cmp_chunks(kv_rows, pe, w1, page_chunks):
    nch = kv_rows.shape[0] // CMP_STRIDE
    n_pages = nch // page_chunks
    tc = page_chunks * max(d for d in range(1, 17) if n_pages % d == 0)
    out = jax.ShapeDtypeStruct((nch, KV_W), F32)
    return pl.pallas_call(
        functools.partial(_cmp_chunk_kernel, tc=tc), name="cmp_chunks",
        grid=(KV_W // HEAD_DIM, nch // tc),
        in_specs=[pl.BlockSpec((tc * CMP_STRIDE, HEAD_DIM), lambda c, i: (i, c)),
                  pl.BlockSpec((1, CMP_LEN, HEAD_DIM), lambda c, i: (c // N_KV, 0, 0)),
                  pl.BlockSpec((1, 2, CMP_STRIDE * HEAD_DIM, CMP_HID), lambda c, i: (c // N_KV, 0, 0, 0))],
        out_specs=[pl.BlockSpec((tc, HEAD_DIM), lambda c, i: (i, c)), pl.BlockSpec((tc, HEAD_DIM), lambda c, i: (i, c))],
        out_shape=[out, out],
        compiler_params=_params("arbitrary", "arbitrary"),
    )(kv_rows, pe, w1)


def _cmp_chunk_flat_kernel(kv_ref, pe_ref, w1_ref, f_ref, s_ref, *, tc):
    n_kv_rows = KV_W // HEAD_DIM
    for c in range(n_kv_rows):
        kind = c // N_KV
        xs = [kv_ref[pl.ds(l * n_kv_rows + c, tc, stride=CMP_STRIDE * n_kv_rows), :] for l in range(CMP_STRIDE)]
        for half, dst in ((0, f_ref), (1, s_ref)):
            xcat = jnp.concatenate(
                [(xs[l] + pe_ref[kind, half * CMP_STRIDE + l:half * CMP_STRIDE + l + 1, :]).astype(BF16)
                 for l in range(CMP_STRIDE)], axis=1)
            dst[:, c * HEAD_DIM:(c + 1) * HEAD_DIM] = jnp.dot(xcat, w1_ref[kind, half], preferred_element_type=F32)


def _cmp_chunks_flat(kv_flat, pe, w1, *, layer, n_pool):
    page_chunks = PAGE_SIZE // CMP_STRIDE
    nch = n_pool * page_chunks
    tc = page_chunks * max(d for d in range(1, 17) if n_pool % d == 0)
    steps = nch // tc
    rows = tc * CMP_STRIDE * (KV_W // HEAD_DIM)
    out = jax.ShapeDtypeStruct((nch, KV_W), F32)
    return pl.pallas_call(
        functools.partial(_cmp_chunk_flat_kernel, tc=tc), name="cmp_chunks_pool",
        grid=(steps,),
        in_specs=[pl.BlockSpec((rows, HEAD_DIM), lambda i: (layer * steps + i, 0)),
                  pl.BlockSpec((2, CMP_LEN, HEAD_DIM), lambda i: (0, 0, 0)),
                  pl.BlockSpec((2, 2, CMP_STRIDE * HEAD_DIM, CMP_HID), lambda i: (0, 0, 0, 0))],
        out_specs=[pl.BlockSpec((tc, KV_W), lambda i: (i, 0)), pl.BlockSpec((tc, KV_W), lambda i: (i, 0))],
        out_shape=[out, out],
        compiler_params=_params("arbitrary"),
    )(kv_flat, pe, w1)


def _cmp_blocks_body(f, s, w2_ref, kc_ref, vc_ref):
    n = f.shape[0]
    hid = _gelu(f + pltpu.roll(s, n - 1, axis=0)).astype(BF16)
    for kind, dst in ((0, kc_ref), (1, vc_ref)):
        outs = [jnp.dot(hid[:, (kind * N_KV + g) * CMP_HID:(kind * N_KV + g + 1) * CMP_HID], w2_ref[kind],
                        preferred_element_type=F32) for g in range(N_KV)]
        dst[0] = jnp.concatenate(outs, axis=1).astype(BF16)


def _cmp_blocks_kernel(f_ref, s_ref, w2_ref, kc_ref, vc_ref):
    _cmp_blocks_body(f_ref[0], s_ref[0], w2_ref, kc_ref, vc_ref)


def _cmp_blocks(f, s, w2):
    b, n, _ = f.shape
    out = jax.ShapeDtypeStruct((b, n, N_KV * HEAD_DIM), BF16)
    blk = pl.BlockSpec((1, n, KV_W), lambda i: (i, 0, 0))
    oblk = pl.BlockSpec((1, n, N_KV * HEAD_DIM), lambda i: (i, 0, 0))
    return pl.pallas_call(
        _cmp_blocks_kernel, name="cmp_blocks",
        grid=(b,), in_specs=[blk, blk, pl.BlockSpec((2, CMP_HID, HEAD_DIM), lambda i: (0, 0, 0))],
        out_specs=[oblk, oblk], out_shape=[out, out],
        compiler_params=_params("arbitrary"),
    )(f, s, w2)


def _cmp_blocks_paged_kernel(pt_ref, *refs, n_pages):
    f_refs, s_refs = refs[:n_pages], refs[n_pages:2 * n_pages]
    w2_ref, kc_ref, vc_ref = refs[2 * n_pages:]
    f = jnp.concatenate([r[0] for r in f_refs], axis=0)
    s = jnp.concatenate([r[0] for r in s_refs], axis=0)
    _cmp_blocks_body(f, s, w2_ref, kc_ref, vc_ref)


def _cmp_blocks_paged(f, s, w2, page_table):
    db, n_pages = page_table.shape
    cpp = f.shape[1]
    n = n_pages * cpp
    out = jax.ShapeDtypeStruct((db, n, N_KV * HEAD_DIM), BF16)
    page = lambda p: pl.BlockSpec((1, cpp, KV_W), lambda b, pt: (pt[b, p], 0, 0))
    oblk = pl.BlockSpec((1, n, N_KV * HEAD_DIM), lambda b, pt: (b, 0, 0))
    grid_spec = pltpu.PrefetchScalarGridSpec(
        num_scalar_prefetch=1, grid=(db,),
        in_specs=[page(p) for p in range(n_pages)] * 2 + [pl.BlockSpec((2, CMP_HID, HEAD_DIM), lambda b, pt: (0, 0, 0))],
        out_specs=[oblk, oblk])
    return pl.pallas_call(
        functools.partial(_cmp_blocks_paged_kernel, n_pages=n_pages), name="cmp_blocks_paged",
        grid_spec=grid_spec, out_shape=[out, out],
        compiler_params=_params("arbitrary"),
    )(page_table, *([f] * n_pages), *([s] * n_pages), w2)


def _stack_heads(q, g):
    return jnp.concatenate([q[:, (g * HPG + h) * HEAD_DIM:(g * HPG + h + 1) * HEAD_DIM] for h in range(HPG)], axis=0)


def _qk(q4, k):
    return lax.dot_general(q4, k, (((1,), (1,)), ((), ())), preferred_element_type=F32)


def _masked_attn(q4, k, v, valid, nq, probs=True):
    nk = k.shape[0]
    sm = jnp.where(valid[None], _qk(q4, k).reshape(HPG, nq, nk), MASKED)
    m = jnp.maximum(jnp.max(sm, axis=-1, keepdims=True), NEG)
    e = jnp.exp(sm - m)
    den = jnp.sum(e, axis=-1, keepdims=True)
    inv = 1.0 / jnp.where(den > 0.0, den, 1.0)
    if not probs:
        o = jnp.dot(e.reshape(HPG * nq, nk).astype(BF16), v, preferred_element_type=F32)
        return o * inv.reshape(HPG * nq, 1), None
    p = e * inv
    return jnp.dot(p.reshape(HPG * nq, nk).astype(BF16), v, preferred_element_type=F32), p


def _compressed_attn(q4, kc, vc, qpos, nq, n_cmp):
    ncp = kc.shape[0]
    ci = lax.broadcasted_iota(jnp.int32, (nq, ncp), 1)
    ok_c = (ci * CMP_STRIDE + (CMP_LEN - 1) <= qpos) & (ci < n_cmp)
    o_c, p = _masked_attn(q4, kc, vc, ok_c, nq)
    return o_c, jnp.sum(p, axis=0)


def _select_blocks(imp, msel_t, qpos_row):
    imp_s = lax.dot_general(msel_t, imp, (((1,), (1,)), ((), ())), preferred_element_type=F32,
                            precision=lax.Precision.HIGHEST)
    blk = lax.broadcasted_iota(jnp.int32, (LANES, LANES), 0)
    cur = qpos_row // SEL_BLOCK
    visible = blk * SEL_BLOCK <= qpos_row
    forced = ((blk == 0) | (blk == cur) | (blk == cur - 1)) & visible
    sel = jnp.where(forced, 1.0, 0.0)
    score = jnp.where(visible & ~forced, imp_s, -BIG)
    for _ in range(N_SEL - N_FORCED):
        top = jnp.max(score, axis=0, keepdims=True)
        idx = jnp.min(jnp.where(score == top, blk, LANES), axis=0, keepdims=True)
        pick = blk == idx
        sel = jnp.where(pick & (top > -0.5 * BIG), 1.0, sel)
        score = jnp.where(pick, -3.0 * BIG, score)
    return sel.T


def _gate_mix(gates, o_c, o_s, o_w):
    outs = []
    for h in range(HPG):
        c0 = h * N_NSA_BRANCH
        outs.append(gates[:, c0:c0 + 1] * o_c[h] + gates[:, c0 + 1:c0 + 2] * o_s[h]
                    + gates[:, c0 + 2:c0 + 3] * o_w[h])
    return jnp.concatenate(outs, axis=1)


def _split_heads(o, nq):
    return [o[h * nq:(h + 1) * nq] for h in range(HPG)]


def _online_step(q4, k, v, valid, carry, nq):
    m, l, acc = carry
    nk = k.shape[0]
    sm = jnp.where(valid[None], _qk(q4, k).reshape(HPG, nq, nk), MASKED)
    m_new = jnp.maximum(m, jnp.max(sm, axis=-1, keepdims=True))
    alpha = jnp.exp(m - m_new)
    e = jnp.exp(sm - m_new)
    l = alpha * l + jnp.sum(e, axis=-1, keepdims=True)
    pv = jnp.dot(e.reshape(HPG * nq, nk).astype(BF16), v, preferred_element_type=F32)
    return m_new, l, alpha * acc + pv.reshape(HPG, nq, HEAD_DIM)


def _nsa_prompt_kernel(q_ref, kc_ref, vc_ref, ks_ref, vs_ref, kw_ref, vw_ref, gates_ref, msel_ref, expand_ref, o_ref,
                       *, seq, n_cmp, tk, wspan):
    j = pl.program_id(2)
    nq = Q_BLOCK
    start = j * Q_BLOCK
    qpos = start + lax.broadcasted_iota(jnp.int32, (nq, 1), 0)
    q4 = _stack_heads(q_ref[...], 0)
    o_c, imp = _compressed_attn(q4, kc_ref[0], vc_ref[0], qpos, nq, n_cmp)
    sel_b = _select_blocks(imp, msel_ref[...], start + lax.broadcasted_iota(jnp.int32, (1, nq), 1)).astype(BF16)

    def tile(kt, carry, diagonal):
        k0 = pl.multiple_of(kt * tk, tk)
        picked = jnp.dot(sel_b, expand_ref[:, pl.ds(k0, tk)], preferred_element_type=F32) > 0.5
        if diagonal:
            picked = picked & (k0 + lax.broadcasted_iota(jnp.int32, (nq, tk), 1) <= qpos)
        return _online_step(q4, ks_ref[pl.ds(k0, tk), :], vs_ref[pl.ds(k0, tk), :], picked, carry, nq)

    last = (start + Q_BLOCK - 1) // tk
    init = (jnp.full((HPG, nq, 1), NEG, F32), jnp.zeros((HPG, nq, 1), F32), jnp.zeros((HPG, nq, HEAD_DIM), F32))
    carry = lax.fori_loop(0, last, lambda kt, c: tile(kt, c, False), init)
    _, l, acc = tile(last, carry, True)
    o_s = (acc * (1.0 / jnp.where(l > 0.0, l, 1.0))).reshape(HPG * nq, HEAD_DIM)

    w0 = pl.multiple_of(jnp.clip(start - WINDOW, 0, seq - wspan), Q_BLOCK)
    kwpos = w0 + lax.broadcasted_iota(jnp.int32, (nq, wspan), 1)
    ok_w = (kwpos <= qpos) & (kwpos > qpos - WINDOW)
    o_w, _ = _masked_attn(q4, kw_ref[pl.ds(w0, wspan), :], vw_ref[pl.ds(w0, wspan), :], ok_w, nq, probs=False)
    o_ref[...] = _gate_mix(gates_ref[...], _split_heads(o_c, nq), _split_heads(o_s, nq), _split_heads(o_w, nq))


def _nsa_prompt(q, kc, vc, kvs_b, kvw_b, gates, msel, expand, *, batch, seq):
    m = q.shape[0]
    nj = seq // Q_BLOCK
    ncp = kc.shape[1]
    tk = min(SEL_KEY_TILE, seq)
    wspan = min(WINDOW + Q_BLOCK, seq)
    gw = HPG * HEAD_DIM
    kcol = lambda c: pl.BlockSpec((seq, HEAD_DIM), lambda b, g, j: (b, c * N_KV + g))
    return pl.pallas_call(
        functools.partial(_nsa_prompt_kernel, seq=seq, n_cmp=seq // CMP_STRIDE - 1, tk=tk, wspan=wspan),
        name="nsa_prompt",
        grid=(batch, N_KV, nj),
        in_specs=[pl.BlockSpec((Q_BLOCK, gw), lambda b, g, j: (b * nj + j, g)),
                  pl.BlockSpec((1, ncp, HEAD_DIM), lambda b, g, j: (b, 0, g)),
                  pl.BlockSpec((1, ncp, HEAD_DIM), lambda b, g, j: (b, 0, g)),
                  kcol(0), kcol(1), kcol(0), kcol(1),
                  pl.BlockSpec((Q_BLOCK, LANES), lambda b, g, j: (b * nj + j, g)),
                  pl.BlockSpec((LANES, ncp), lambda b, g, j: (0, 0)),
                  pl.BlockSpec((LANES, seq), lambda b, g, j: (0, 0))],
        out_specs=pl.BlockSpec((Q_BLOCK, gw), lambda b, g, j: (b * nj + j, g)),
        out_shape=jax.ShapeDtypeStruct((m, N_HEADS * HEAD_DIM), F32),
        compiler_params=_params("arbitrary", "arbitrary", "arbitrary"),
    )(q, kc, vc, kvs_b, kvs_b, kvw_b, kvw_b, gates, msel, expand)


def _nsa_sample_kernel(pt_ref, *refs, n_pages, past, wbuf, ts):
    page_refs = refs[:n_pages]
    (q_ref, kc_ref, vc_ref, ksn_ref, win_ref, kwn_ref, gates_ref, msel_ref, expand_ref,
     o_ref, wnew_ref) = refs[n_pages:]
    nq = SAMPLE_Q_PAD
    n_kv_rows = KV_W // HEAD_DIM
    n_cached = n_pages * PAGE_SIZE
    qpos = past + lax.broadcasted_iota(jnp.int32, (nq, 1), 0)
    zeros = jnp.zeros((PAGE_SIZE - nq, HEAD_DIM), BF16)
    kpos = lax.broadcasted_iota(jnp.int32, (nq, n_cached + PAGE_SIZE), 1)
    ok_s = (kpos <= qpos) & (kpos < past + nq)
    wpos = past - wbuf + lax.broadcasted_iota(jnp.int32, (nq, wbuf + PAGE_SIZE), 1)
    ok_w = (wpos <= qpos) & (wpos > qpos - WINDOW) & (wpos >= 0) & (wpos < past + nq)

    q4s, o_cs, imps = [], [], []
    for g in range(N_KV):
        gcols = slice(g * HEAD_DIM, (g + 1) * HEAD_DIM)
        q4s.append(_stack_heads(q_ref[...], g))
        o_c, imp = _compressed_attn(q4s[g], kc_ref[0][:, gcols], vc_ref[0][:, gcols], qpos, nq,
                                    n_cached // CMP_STRIDE - 1)
        o_cs.append(o_c)
        imps.append(imp)
    imp_all = jnp.concatenate(imps + [jnp.zeros((LANES - N_KV * nq, imps[0].shape[1]), F32)], axis=0)
    qpos_row = past + (lax.broadcasted_iota(jnp.int32, (1, LANES), 1) & (nq - 1))
    sel_all = _select_blocks(imp_all, msel_ref[...], qpos_row)

    def cached_rows(ref, n, c):
        return ref[pl.ds(c, n, stride=n_kv_rows), :].astype(BF16)

    def with_new(cached, new_ref, c):
        return jnp.concatenate(cached + [new_ref[:, c * HEAD_DIM:(c + 1) * HEAD_DIM].astype(BF16), zeros], axis=0)

    outs = []
    for g in range(N_KV):
        q4 = q4s[g]
        picked = jnp.dot(sel_all[g * nq:(g + 1) * nq].astype(BF16), expand_ref[...], preferred_element_type=F32) > 0.5
        k_all = with_new([cached_rows(r, PAGE_SIZE, g) for r in page_refs], ksn_ref, g)
        v_all = with_new([cached_rows(r, PAGE_SIZE, N_KV + g) for r in page_refs], ksn_ref, N_KV + g)
        o_s, _ = _masked_attn(q4, k_all, v_all, picked & ok_s, nq, probs=False)
        kw_all = with_new([cached_rows(win_ref, wbuf, g)], kwn_ref, g)
        vw_all = with_new([cached_rows(win_ref, wbuf, N_KV + g)], kwn_ref, N_KV + g)
        o_w, _ = _masked_attn(q4, kw_all, vw_all, ok_w, nq, probs=False)
        outs.append(_gate_mix(gates_ref[:, g * LANES:(g + 1) * LANES], _split_heads(o_cs[g], nq),
                              _split_heads(o_s, nq), _split_heads(o_w, nq)))
    o_ref[...] = jnp.concatenate(outs, axis=1)

    keep = (wbuf - ts) * n_kv_rows
    wnew_ref[0:keep, :] = win_ref[ts * n_kv_rows:wbuf * n_kv_rows, :]
    wnew_ref[keep:wbuf * n_kv_rows, :] = jnp.concatenate(
        [kwn_ref[t:t + 1, c * HEAD_DIM:(c + 1) * HEAD_DIM] for t in range(ts) for c in range(n_kv_rows)], axis=0)


def _nsa_sample(q8, kc, vc, slc_flat, kvs8, win_flat, kvw8, gates8, msel, expand, page_table, *,
                layer, n_pool, wbuf, past, ts):
    db, n_pages = page_table.shape
    nq = SAMPLE_Q_PAD
    n = kc.shape[1]
    n_kv_rows = KV_W // HEAD_DIM
    page = lambda p: pl.BlockSpec((PAGE_SIZE * n_kv_rows, HEAD_DIM), lambda b, pt: (layer * n_pool + pt[b, p], 0))
    rows = lambda w: pl.BlockSpec((nq, w), lambda b, pt: (b, 0))
    grid_spec = pltpu.PrefetchScalarGridSpec(
        num_scalar_prefetch=1, grid=(db,),
        in_specs=[page(p) for p in range(n_pages)] + [
            rows(N_HEADS * HEAD_DIM),
            pl.BlockSpec((1, n, N_KV * HEAD_DIM), lambda b, pt: (b, 0, 0)),
            pl.BlockSpec((1, n, N_KV * HEAD_DIM), lambda b, pt: (b, 0, 0)),
            rows(KV_W),
            pl.BlockSpec((wbuf * n_kv_rows, HEAD_DIM), lambda b, pt: (layer * db + b, 0)),
            rows(KV_W), rows(N_KV * LANES),
            pl.BlockSpec((LANES, n), lambda b, pt: (0, 0)),
            pl.BlockSpec((LANES, n_pages * PAGE_SIZE + PAGE_SIZE), lambda b, pt: (0, 0))],
        out_specs=[rows(N_HEADS * HEAD_DIM), pl.BlockSpec((wbuf * n_kv_rows, HEAD_DIM), lambda b, pt: (b, 0))])
    return pl.pallas_call(
        functools.partial(_nsa_sample_kernel, n_pages=n_pages, past=past, wbuf=wbuf, ts=ts), name="nsa_sample",
        grid_spec=grid_spec,
        out_shape=[jax.ShapeDtypeStruct((db * nq, N_HEADS * HEAD_DIM), F32),
                   jax.ShapeDtypeStruct((db * wbuf * n_kv_rows, HEAD_DIM), F32)],
        compiler_params=_params("arbitrary"),
    )(page_table, *([slc_flat] * n_pages), q8, kc, vc, kvs8, win_flat, kvw8, gates8, msel, expand)


def _merge_kernel(a_ref, c_ref, o_ref, mg_ref, x_ref, wb_ref, wo_ref, lng_ref, lnb_ref, gpost_ref, y_ref):
    c = c_ref[...]
    mu = jnp.mean(c, axis=-1, keepdims=True)
    var = jnp.mean(jnp.square(c - mu), axis=-1, keepdims=True)
    n = (c - mu) * lax.rsqrt(var + EPS) * lng_ref[...] + lnb_ref[...]
    branches = (a_ref[...], n * _sigmoid(n), o_ref[...])
    merged = None
    for i, br in enumerate(branches):
        proj = jnp.dot(br.astype(BF16), wb_ref[i], preferred_element_type=F32)
        term = mg_ref[:, i * D_MODEL:(i + 1) * D_MODEL] * proj
        merged = term if merged is None else merged + term
    mix = jnp.dot(merged.astype(BF16), wo_ref[...], preferred_element_type=F32)
    y_ref[...] = x_ref[...] + _rms(mix, gpost_ref[...])


def _merge(a, c, o, mg, x, wb, wo, lng, lnb, gpost, tm):
    m = x.shape[0]
    row = lambda w: pl.BlockSpec((tm, w), lambda i: (i, 0))
    vec = pl.BlockSpec((1, D_MODEL), lambda i: (0, 0))
    return pl.pallas_call(
        _merge_kernel, name="merge",
        grid=(m // tm,),
        in_specs=[row(D_MODEL), row(D_MODEL), row(D_MODEL), row(N_BRANCH * D_MODEL), row(D_MODEL),
                  pl.BlockSpec((N_BRANCH, LRU_WIDTH, D_MODEL), lambda i: (0, 0, 0)),
                  pl.BlockSpec((D_MODEL, D_MODEL), lambda i: (0, 0)), vec, vec, vec],
        out_specs=row(D_MODEL), out_shape=jax.ShapeDtypeStruct((m, D_MODEL), F32),
        compiler_params=_params("arbitrary"),
    )(a, c, o, mg, x, wb, wo, lng, lnb, gpost)


def _ffn_in_kernel(x_ref, g_ref, w_ref, gate_ref, up_ref, h_ref):
    j = pl.program_id(1)
    n_gate = D_FF // FFN_COL_TILE

    @pl.when(j == 0)
    def _():
        h_ref[...] = _rms(x_ref[...], g_ref[...]).astype(BF16)

    z = jnp.dot(h_ref[...], w_ref[...], preferred_element_type=F32)

    @pl.when(j < n_gate)
    def _():
        gate_ref[...] = z

    @pl.when(j >= n_gate)
    def _():
        up_ref[...] = z


def _ffn_in(x, g, w, tm):
    m = x.shape[0]
    n_gate = D_FF // FFN_COL_TILE
    out = jax.ShapeDtypeStruct((m, D_FF), F32)
    return pl.pallas_call(
        _ffn_in_kernel, name="ffn_in",
        grid=(m // tm, 2 * n_gate),
        in_specs=[pl.BlockSpec((tm, D_MODEL), lambda i, j: (i, 0)), pl.BlockSpec((1, D_MODEL), lambda i, j: (0, 0)),
                  pl.BlockSpec((D_MODEL, FFN_COL_TILE), lambda i, j: (0, j))],
        out_specs=[pl.BlockSpec((tm, FFN_COL_TILE), lambda i, j: (i, jnp.minimum(j, n_gate - 1))),
                   pl.BlockSpec((tm, FFN_COL_TILE), lambda i, j: (i, jnp.maximum(j - n_gate, 0)))],
        out_shape=[out, out],
        scratch_shapes=[pltpu.VMEM((tm, D_MODEL), BF16)],
        compiler_params=_params("arbitrary", "arbitrary"),
    )(x, g, w)


def _ffn_out_kernel(gate_ref, up_ref, x_ref, buf0_ref, cw_ref, cb_ref, wo_ref, gpost_ref, y_ref, bufnew_ref, gbuf,
                    *, R, tt):
    halo = (FFN_CONV - 1) * R
    p0 = _halo_pad(halo)

    @pl.when(pl.program_id(1) == 0)
    def _():
        gbuf[p0:p0 + halo, :] = buf0_ref[0]

    gbuf[p0 + halo:p0 + halo + tt, :] = gate_ref[...]
    c = jnp.broadcast_to(cb_ref[...], (tt, D_FF))
    for k in range(FFN_CONV):
        c = c + cw_ref[k:k + 1, :] * gbuf[p0 + k * R:p0 + k * R + tt, :]
    act = (_gelu(c) * up_ref[...]).astype(BF16)
    f = jnp.dot(act, wo_ref[...], preferred_element_type=F32)
    y_ref[...] = x_ref[...] + _rms(f, gpost_ref[...])
    tail = gbuf[p0 + tt:p0 + tt + halo, :]
    bufnew_ref[0] = tail
    gbuf[p0:p0 + halo, :] = tail


def _ffn_out(gate, up, x, buf0, cw, cb, wo, gpost, *, n_seq, R, tt):
    m = x.shape[0]
    nt = m // n_seq // tt
    halo = (FFN_CONV - 1) * R
    row = lambda w: pl.BlockSpec((tt, w), lambda s, i: (s * nt + i, 0))
    full2 = lambda s, i: (0, 0)
    seq = pl.BlockSpec((1, halo, D_FF), lambda s, i: (s, 0, 0))
    return pl.pallas_call(
        functools.partial(_ffn_out_kernel, R=R, tt=tt), name="ffn_out",
        grid=(n_seq, nt),
        in_specs=[row(D_FF), row(D_FF), row(D_MODEL), seq,
                  pl.BlockSpec((FFN_CONV, D_FF), full2), pl.BlockSpec((1, D_FF), full2),
                  pl.BlockSpec((D_FF, D_MODEL), full2), pl.BlockSpec((1, D_MODEL), full2)],
        out_specs=[row(D_MODEL), seq],
        out_shape=[jax.ShapeDtypeStruct((m, D_MODEL), F32), jax.ShapeDtypeStruct((n_seq, halo, D_FF), F32)],
        scratch_shapes=[pltpu.VMEM((_halo_pad(halo) + halo + tt, D_FF), F32)],
        compiler_params=_params("arbitrary", "arbitrary"),
    )(gate, up, x, buf0, cw, cb, wo, gpost)


def _rope_tables(pos):
    half = HEAD_DIM // 2
    inv = ROPE_THETA ** (-jnp.arange(half, dtype=F32) / half)
    ang = pos.astype(F32)[:, None] * inv[None, :]
    cos, sin = jnp.cos(ang), jnp.sin(ang)
    return jnp.concatenate([cos, cos], axis=1), jnp.concatenate([-sin, sin], axis=1)


def _select_matrix(ncp):
    i = np.arange(ncp)[None, :]
    s = np.arange(LANES)[:, None]
    return jnp.asarray((i // CH_PER_SEL == s).astype(np.float32) + ((i + 1) // CH_PER_SEL == s).astype(np.float32))


def _expand_matrix(n_keys):
    return jnp.asarray(np.arange(n_keys)[None, :] // SEL_BLOCK == np.arange(LANES)[:, None], dtype=BF16)


def _layer_weights(l, norm_pre_mix, norm_post_mix, norm_pre_ffn, norm_post_ffn, w_in, lru_conv_w, lru_conv_b,
                   lru_gate_w, lru_gate_b, lru_lambda, conf_conv_w, conf_conv_b, conf_ln_g, conf_ln_b, cmp_pe,
                   cmp_w1, cmp_w2, w_branch, w_out, ffn_w_in, ffn_conv_w, ffn_conv_b, ffn_w_out):
    w = w_in[l]
    per_group = _NG_W // N_KV
    ng = jnp.concatenate([jnp.pad(w[:, _NG_OFF + g * per_group:_NG_OFF + (g + 1) * per_group],
                                  ((0, 0), (0, LANES - per_group))) for g in range(N_KV)], axis=1)
    ng = jnp.pad(ng, ((0, 0), (0, COL_TILE - N_KV * LANES)))
    w_perm = jnp.concatenate([w[:, :_NG_OFF], w[:, _NG_OFF + _NG_W:], ng], axis=1).astype(BF16)
    row = lambda v: v[l].reshape(1, -1)
    return dict(
        g_pre_mix=row(norm_pre_mix), g_post_mix=row(norm_post_mix), g_pre_ffn=row(norm_pre_ffn),
        g_post_ffn=row(norm_post_ffn), w_in=w_perm,
        lru_cw=lru_conv_w[l], lru_cb=row(lru_conv_b), lru_gw=lru_gate_w[l].astype(BF16),
        lru_gb=lru_gate_b[l].reshape(2, LRU_WIDTH), lru_lsl=jax.nn.log_sigmoid(lru_lambda[l].astype(F32)).reshape(1, -1),
        conf_cw=conf_conv_w[l], conf_cb=row(conf_conv_b), ln_g=row(conf_ln_g), ln_b=row(conf_ln_b),
        pe=cmp_pe[l], w1=cmp_w1[l].reshape(2, 2, CMP_STRIDE * HEAD_DIM, CMP_HID).astype(BF16),
        w2=cmp_w2[l].astype(BF16), wb=w_branch[l].astype(BF16), wo=w_out[l].astype(BF16),
        ffn_wi=ffn_w_in[l].astype(BF16), ffn_cw=ffn_conv_w[l], ffn_cb=row(ffn_conv_b),
        ffn_wo=ffn_w_out[l].astype(BF16))


def _layer(x, lw, cos, sin, state, attend, *, n_seq, R, tm, tt, conf_ct):
    lru_h0, lru_buf, conf_buf, ffn_buf = state
    xr, yr, u, q, kvc, kvs, kvw, kvs_b, kvw_b, mg, ng = _in_proj(x, lw["g_pre_mix"], lw["w_in"], cos, sin, tm)
    a_out, lru_h, lru_buf_new = _lru(xr, yr, lru_buf, lru_h0, lw["lru_cw"], lw["lru_cb"], lw["lru_gw"], lw["lru_gb"],
                                     lw["lru_lsl"], n_seq=n_seq, R=R, tt=tt)
    conv, conf_buf_new = _conf_conv(u, conf_buf, lw["conf_cw"], lw["conf_cb"], n_seq=n_seq, R=R,
                                    tt=x.shape[0] // n_seq if conf_ct < CONF_CH else tt, ct=conf_ct)
    o, win_new = attend(q, kvc, kvs, kvw, kvs_b, kvw_b, ng)
    x = _merge(a_out, conv, o, mg, x, lw["wb"], lw["wo"], lw["ln_g"], lw["ln_b"], lw["g_post_mix"], min(tm, 256))
    gate, up = _ffn_in(x, lw["g_pre_ffn"], lw["ffn_wi"], tm)
    x, ffn_buf_new = _ffn_out(gate, up, x, ffn_buf, lw["ffn_cw"], lw["ffn_cb"], lw["ffn_wo"], lw["g_post_ffn"],
                              n_seq=n_seq, R=R, tt=min(tt, 256) if R == 1 else R)
    return x, (kvc, kvs, win_new, lru_h, lru_buf_new, conf_buf_new, ffn_buf_new)


def kernel(x_prompt, x_sample, cache_cmp_kv, cache_slc_kv, cache_win_kv, state_lru_h, state_lru_conv, state_conf_conv, state_ffn_conv, page_table, norm_pre_mix, norm_post_mix, norm_pre_ffn, norm_post_ffn, w_in, lru_conv_w, lru_conv_b, lru_gate_w, lru_gate_b, lru_lambda, conf_conv_w, conf_conv_b, conf_ln_g, conf_ln_b, cmp_pe, cmp_w1, cmp_w2, w_branch, w_out, ffn_w_in, ffn_conv_w, ffn_conv_b, ffn_w_out):
    B, S, _ = x_prompt.shape
    DB, TS, _ = x_sample.shape
    depth = w_in.shape[0]
    n_pages = page_table.shape[1]
    past = n_pages * PAGE_SIZE
    n_pool = cache_cmp_kv.shape[1]
    wbuf = cache_win_kv.shape[2]
    assert S % SEL_KEY_TILE == 0 and TS <= SAMPLE_Q_PAD and DB % SUBLANES == 0 and wbuf == WINDOW
    kv_row = (2, N_KV, HEAD_DIM)
    cmp_flat = cache_cmp_kv.reshape(-1, HEAD_DIM)
    slc_flat = cache_slc_kv.reshape(-1, HEAD_DIM)
    win_flat = cache_win_kv.reshape(-1, HEAD_DIM)

    cos_p, sin_p = _rope_tables(jnp.arange(S))
    cos_s, sin_s = _rope_tables(past + jnp.repeat(jnp.arange(TS), DB))
    msel_p = _select_matrix(S // CMP_STRIDE)
    msel_s = _select_matrix(past // CMP_STRIDE)
    expand_p = _expand_matrix(S)
    expand_s = _expand_matrix(past + PAGE_SIZE)
    tm_p = 1024
    tm_s = DB * TS
    to_tb = lambda a: jnp.swapaxes(a, 0, 1).reshape((TS * DB,) + a.shape[2:])
    to_bt = lambda a: jnp.swapaxes(a.reshape((TS, DB) + a.shape[1:]), 0, 1)
    pad_q = lambda a: jnp.pad(to_bt(a), ((0, 0), (0, SAMPLE_Q_PAD - TS), (0, 0))).reshape(DB * SAMPLE_Q_PAD, -1)
    state_in = lambda a: jnp.swapaxes(a, 0, 1).reshape(1, -1, a.shape[-1])
    state_out = lambda a, k: jnp.swapaxes(a.reshape(k, DB, a.shape[-1]), 0, 1)

    xp = x_prompt.reshape(B * S, D_MODEL)
    xs = to_tb(x_sample)
    st_p, st_s = [], []
    for l in range(depth):
        lw = _layer_weights(l, norm_pre_mix, norm_post_mix, norm_pre_ffn, norm_post_ffn, w_in, lru_conv_w,
                            lru_conv_b, lru_gate_w, lru_gate_b, lru_lambda, conf_conv_w, conf_conv_b, conf_ln_g,
                            conf_ln_b, cmp_pe, cmp_w1, cmp_w2, w_branch, w_out, ffn_w_in, ffn_conv_w, ffn_conv_b,
                            ffn_w_out)

        def attend_p(q, kvc, kvs, kvw, kvs_b, kvw_b, ng, lw=lw):
            f, s = _cmp_chunks(kvc, lw["pe"], lw["w1"], PAGE_SIZE // CMP_STRIDE)
            n = S // CMP_STRIDE
            kc, vc = _cmp_blocks(f.reshape(B, n, KV_W), s.reshape(B, n, KV_W), lw["w2"])
            o = _nsa_prompt(q, kc, vc, kvs_b, kvw_b, ng, msel_p, expand_p, batch=B, seq=S)
            return o, kvw.reshape((B, S) + kv_row)[:, S - min(WINDOW, S):]

        zeros = lambda k, c: jnp.zeros((B, k, c), F32)
        xp, sp = _layer(xp, lw, cos_p, sin_p,
                        (zeros(1, LRU_WIDTH), zeros(LRU_CONV - 1, LRU_WIDTH), zeros(CONF_CONV - 1, CONF_CH),
                         zeros(FFN_CONV - 1, D_FF)),
                        attend_p, n_seq=B, R=1, tm=tm_p, tt=256, conf_ct=CONF_CH)

        def attend_s(q, kvc, kvs, kvw, kvs_b, kvw_b, ng, lw=lw, l=l):
            cpp = PAGE_SIZE // CMP_STRIDE
            f, s = _cmp_chunks_flat(cmp_flat, lw["pe"], lw["w1"], layer=l, n_pool=n_pool)
            kc, vc = _cmp_blocks_paged(f.reshape(n_pool, cpp, KV_W), s.reshape(n_pool, cpp, KV_W), lw["w2"],
                                       page_table)
            o8, win_new = _nsa_sample(pad_q(q), kc, vc, slc_flat, pad_q(kvs), win_flat, pad_q(kvw), pad_q(ng),
                                      msel_s, expand_s, page_table, layer=l, n_pool=n_pool, wbuf=wbuf, past=past,
                                      ts=TS)
            return to_tb(o8.reshape(DB, SAMPLE_Q_PAD, -1)[:, :TS]), win_new.reshape((DB, wbuf) + kv_row)

        xs, ss = _layer(xs, lw, cos_s, sin_s,
                        (state_lru_h[l][None], state_in(state_lru_conv[l]), state_in(state_conf_conv[l]),
                         state_in(state_ffn_conv[l])),
                        attend_s, n_seq=1, R=DB, tm=tm_s, tt=DB, conf_ct=256)
        st_p.append(sp)
        st_s.append(ss)

    def kv_p(i):
        return jnp.stack([s[i].reshape((B, S) + kv_row) for s in st_p])

    def kv_s(i):
        return jnp.stack([to_bt(s[i]).reshape((DB, TS) + kv_row) for s in st_s])

    return (xp.reshape(B, S, D_MODEL), to_bt(xs),
            kv_p(0), kv_s(0), kv_p(1), kv_s(1), jnp.stack([s[2] for s in st_p]), jnp.stack([s[2] for s in st_s]),
            jnp.stack([s[3][:, 0] for s in st_p]), jnp.stack([s[3][0] for s in st_s]),
            jnp.stack([s[4] for s in st_p]), jnp.stack([state_out(s[4], LRU_CONV - 1) for s in st_s]),
            jnp.stack([s[5] for s in st_p]), jnp.stack([state_out(s[5], CONF_CONV - 1) for s in st_s]),
            jnp.stack([s[6] for s in st_p]), jnp.stack([state_out(s[6], FFN_CONV - 1) for s in st_s]))
```

```python
import functools

import jax
import jax.numpy as jnp
import numpy as np
from jax import lax
from jax.experimental import pallas as pl
from jax.experimental.pallas import tpu as pltpu

F32 = jnp.float32
BF16 = jnp.bfloat16

D_MODEL = 1024
PAGE_SIZE = 128
LRU_WIDTH = 1024
LRU_BLOCKS = 8
LRU_BLOCK_W = LRU_WIDTH // LRU_BLOCKS
LRU_CONV = 4
LRU_C = 8.0
CONF_CH = 1024
CONF_CONV = 31
N_HEADS = 8
N_KV = 2
HPG = N_HEADS // N_KV
HEAD_DIM = 128
CMP_LEN = 32
CMP_STRIDE = 16
CMP_HID = 128
SEL_BLOCK = 64
N_SEL = 16
N_FORCED = 3
CH_PER_SEL = SEL_BLOCK // CMP_STRIDE
WINDOW = 512
Q_BLOCK = 128
N_NSA_BRANCH = 3
ROPE_THETA = 10000.0
N_BRANCH = 3
D_FF = 3072
FFN_CONV = 3
EPS = 1e-6
NEG = -1e30
MASKED = 2.0 * NEG
BIG = 1e9

LANES = 128
SUBLANES = 8
KV_W = 2 * N_KV * HEAD_DIM
COL_TILE = 512
FFN_COL_TILE = 1024
SEL_KEY_TILE = 512
SAMPLE_Q_PAD = 8
CONV_ACC_VREGS = 32
VMEM_LIMIT = 56 * 1024 * 1024

_T_XR, _T_YR, _T_U, _T_Q, _T_KVC, _T_KVS, _T_KVW, _T_MG, _T_NG = 0, 2, 4, 8, 10, 11, 12, 13, 19
_N_IN_TILES = 20
_NG_OFF = LRU_WIDTH * 2 + 2 * CONF_CH + N_HEADS * HEAD_DIM + 3 * KV_W
_NG_W = N_NSA_BRANCH * N_HEADS


def _params(*sem):
    return pltpu.CompilerParams(dimension_semantics=sem, vmem_limit_bytes=VMEM_LIMIT)


def _gelu(x):
    return 0.5 * x * (1.0 + jnp.tanh(0.7978845608028654 * (x + 0.044715 * (x * x * x))))


def _sigmoid(x):
    return 1.0 / (1.0 + jnp.exp(-x))


def _rms(x, g):
    return x * lax.rsqrt(jnp.mean(x * x, axis=-1, keepdims=True) + EPS) * g


def _rope_heads(z, cos, sin, nheads):
    outs = []
    for h in range(nheads):
        xh = z[:, h * HEAD_DIM:(h + 1) * HEAD_DIM]
        outs.append(xh * cos + pltpu.roll(xh, HEAD_DIM // 2, axis=1) * sin)
    return outs


def _in_proj_kernel(x_ref, g_ref, w_ref, cos_ref, sin_ref,
                    xr_ref, yr_ref, u_ref, q_ref, kvc_ref, kvs_ref, kvw_ref, kvsb_ref, kvwb_ref,
                    mg_ref, ng_ref, h_ref, *, tm, flat_kv):
    j = pl.program_id(1)
    n_kv_rows = KV_W // HEAD_DIM

    def store_kv(ref, kv):
        if flat_kv:
            for c in range(n_kv_rows):
                ref[pl.ds(c, tm, stride=n_kv_rows), :] = kv[:, c * HEAD_DIM:(c + 1) * HEAD_DIM]
        else:
            ref[...] = kv

    @pl.when(j == 0)
    def _():
        h_ref[...] = _rms(x_ref[...], g_ref[...]).astype(BF16)

    z = jnp.dot(h_ref[...], w_ref[...], preferred_element_type=F32)

    @pl.when(j < _T_YR)
    def _():
        xr_ref[...] = z

    @pl.when((j >= _T_YR) & (j < _T_U))
    def _():
        yr_ref[...] = z

    @pl.when((j >= _T_U) & (j < _T_Q))
    def _():
        u_ref[...] = z

    @pl.when((j >= _T_Q) & (j < _T_KVC))
    def _():
        heads = _rope_heads(z, cos_ref[...], sin_ref[...], COL_TILE // HEAD_DIM)
        q_ref[...] = (jnp.concatenate(heads, axis=1) * HEAD_DIM ** -0.5).astype(BF16)

    def kv_rows():
        heads = _rope_heads(z, cos_ref[...], sin_ref[...], N_KV)
        return jnp.concatenate(heads + [z[:, N_KV * HEAD_DIM:]], axis=1)

    @pl.when(j == _T_KVC)
    def _():
        store_kv(kvc_ref, kv_rows())

    @pl.when(j == _T_KVS)
    def _():
        kv = kv_rows()
        store_kv(kvs_ref, kv)
        kvsb_ref[...] = kv.astype(BF16)

    @pl.when(j == _T_KVW)
    def _():
        kv = kv_rows()
        store_kv(kvw_ref, kv)
        kvwb_ref[...] = kv.astype(BF16)

    @pl.when((j >= _T_MG) & (j < _T_NG))
    def _():
        mg_ref[...] = _sigmoid(z)

    @pl.when(j == _T_NG)
    def _():
        ng_ref[...] = _sigmoid(z[:, :N_KV * LANES])


def _in_proj(x, g, w, cos, sin, tm, flat_kv):
    m = x.shape[0]
    nper = cos.shape[0] // tm
    n_kv_rows = KV_W // HEAD_DIM
    kv_shape = jax.ShapeDtypeStruct((m * n_kv_rows, HEAD_DIM) if flat_kv else (m, KV_W), F32)

    def cols(t0, n):
        return lambda i, j: (i, jnp.clip(j - t0, 0, n - 1))

    tile = lambda t0, n: pl.BlockSpec((tm, COL_TILE), cols(t0, n))
    out_shape = [jax.ShapeDtypeStruct((m, LRU_WIDTH), F32), jax.ShapeDtypeStruct((m, LRU_WIDTH), F32),
                 jax.ShapeDtypeStruct((m, 2 * CONF_CH), F32), jax.ShapeDtypeStruct((m, N_HEADS * HEAD_DIM), BF16),
                 kv_shape, kv_shape, kv_shape, jax.ShapeDtypeStruct((m, KV_W), BF16),
                 jax.ShapeDtypeStruct((m, KV_W), BF16), jax.ShapeDtypeStruct((m, N_BRANCH * D_MODEL), F32),
                 jax.ShapeDtypeStruct((m, N_KV * LANES), F32)]
    kv_tile = (lambda t0: pl.BlockSpec((tm * n_kv_rows, HEAD_DIM), lambda i, j: (i, 0))) if flat_kv else (
        lambda t0: tile(t0, 1))
    out_specs = [tile(_T_XR, 2), tile(_T_YR, 2), tile(_T_U, 4), tile(_T_Q, 2), kv_tile(_T_KVC), kv_tile(_T_KVS),
                 kv_tile(_T_KVW), tile(_T_KVS, 1), tile(_T_KVW, 1), tile(_T_MG, 6),
                 pl.BlockSpec((tm, N_KV * LANES), lambda i, j: (i, 0))]
    return pl.pallas_call(
        functools.partial(_in_proj_kernel, tm=tm, flat_kv=flat_kv), name="in_proj",
        grid=(m // tm, _N_IN_TILES),
        in_specs=[pl.BlockSpec((tm, D_MODEL), lambda i, j: (i, 0)),
                  pl.BlockSpec((1, D_MODEL), lambda i, j: (0, 0)),
                  pl.BlockSpec((D_MODEL, COL_TILE), lambda i, j: (0, j)),
                  pl.BlockSpec((tm, HEAD_DIM), lambda i, j: (i % nper, 0)),
                  pl.BlockSpec((tm, HEAD_DIM), lambda i, j: (i % nper, 0))],
        out_specs=out_specs, out_shape=out_shape,
        scratch_shapes=[pltpu.VMEM((tm, D_MODEL), BF16)],
        compiler_params=_params("arbitrary", "arbitrary"),
    )(x, g, w, cos, sin)


def _halo_pad(halo):
    return (-halo) % SUBLANES


def _lru_kernel(xr_ref, yr_ref, buf0_ref, h0_ref, cw_ref, cb_ref, gw_ref, gb_ref, lsl_ref,
                out_ref, hnew_ref, bufnew_ref, xbuf, a_s, b_s, hcar, *, R, tt):
    halo = (LRU_CONV - 1) * R
    p0 = _halo_pad(halo)

    @pl.when(pl.program_id(1) == 0)
    def _():
        xbuf[p0:p0 + halo, :] = buf0_ref[0]
        hcar[...] = h0_ref[0]

    xbuf[p0 + halo:p0 + halo + tt, :] = xr_ref[...]
    xc = jnp.broadcast_to(cb_ref[...], (tt, LRU_WIDTH))
    for k in range(LRU_CONV):
        xc = xc + cw_ref[k:k + 1, :] * xbuf[p0 + k * R:p0 + k * R + tt, :]
    gi, gr = [], []
    for n in range(LRU_BLOCKS):
        xb = xc[:, n * LRU_BLOCK_W:(n + 1) * LRU_BLOCK_W].astype(BF16)
        gi.append(jnp.dot(xb, gw_ref[0, n], preferred_element_type=F32))
        gr.append(jnp.dot(xb, gw_ref[1, n], preferred_element_type=F32))
    gi = _sigmoid(jnp.concatenate(gi, axis=1) + gb_ref[0:1, :])
    gr = _sigmoid(jnp.concatenate(gr, axis=1) + gb_ref[1:2, :])
    log_a = LRU_C * gr * lsl_ref[...]
    a_s[...] = jnp.exp(log_a)
    th = jnp.tanh(log_a)
    b_s[...] = xc * gi * jnp.sqrt(-2.0 * th / (1.0 - th))

    def step(s, h):
        rows = pl.ds(pl.multiple_of(s * R, R), R)
        h = a_s[rows, :] * h + b_s[rows, :]
        b_s[rows, :] = h
        return h

    h = lax.fori_loop(0, tt // R, step, hcar[...], unroll=8 if tt // R >= 8 else True)
    hcar[...] = h
    hnew_ref[0] = h
    out_ref[...] = b_s[...] * _gelu(yr_ref[...])
    tail = xbuf[p0 + tt:p0 + tt + halo, :]
    bufnew_ref[0] = tail
    xbuf[p0:p0 + halo, :] = tail


def _lru(xr, yr, buf0, h0, cw, cb, gw, gb, lsl, *, n_seq, R, tt):
    m = xr.shape[0]
    nt = m // n_seq // tt
    halo = (LRU_CONV - 1) * R
    row = lambda s, i: (s * nt + i, 0)
    seq = lambda s, i: (s, 0, 0)
    full2 = lambda s, i: (0, 0)
    return pl.pallas_call(
        functools.partial(_lru_kernel, R=R, tt=tt), name="rglru",
        grid=(n_seq, nt),
        in_specs=[pl.BlockSpec((tt, LRU_WIDTH), row), pl.BlockSpec((tt, LRU_WIDTH), row),
                  pl.BlockSpec((1, halo, LRU_WIDTH), seq), pl.BlockSpec((1, R, LRU_WIDTH), seq),
                  pl.BlockSpec((LRU_CONV, LRU_WIDTH), full2), pl.BlockSpec((1, LRU_WIDTH), full2),
                  pl.BlockSpec((2, LRU_BLOCKS, LRU_BLOCK_W, LRU_BLOCK_W), lambda s, i: (0, 0, 0, 0)),
                  pl.BlockSpec((2, LRU_WIDTH), full2), pl.BlockSpec((1, LRU_WIDTH), full2)],
        out_specs=[pl.BlockSpec((tt, LRU_WIDTH), row), pl.BlockSpec((1, R, LRU_WIDTH), seq),
                   pl.BlockSpec((1, halo, LRU_WIDTH), seq)],
        out_shape=[jax.ShapeDtypeStruct((m, LRU_WIDTH), F32), jax.ShapeDtypeStruct((n_seq, R, LRU_WIDTH), F32),
                   jax.ShapeDtypeStruct((n_seq, halo, LRU_WIDTH), F32)],
        scratch_shapes=[pltpu.VMEM((_halo_pad(halo) + halo + tt, LRU_WIDTH), F32),
                        pltpu.VMEM((tt, LRU_WIDTH), F32), pltpu.VMEM((tt, LRU_WIDTH), F32),
                        pltpu.VMEM((R, LRU_WIDTH), F32)],
        compiler_params=_params("arbitrary", "arbitrary"),
    )(xr, yr, buf0, h0, cw, cb, gw, gb, lsl)


def _conf_kernel(ua_ref, ub_ref, buf0_ref, cw_ref, cb_ref, out_ref, bufnew_ref, xbuf, *phase_buf, R, tt, ct, rb):
    halo = (CONF_CONV - 1) * R
    p0 = _halo_pad(halo)

    @pl.when(pl.program_id(2) == 0)
    def _():
        xbuf[p0:p0 + halo, :] = buf0_ref[0]

    xbuf[p0 + halo:p0 + halo + tt, :] = ua_ref[...] * _sigmoid(ub_ref[...])
    offs = [p0 + k * R for k in range(CONF_CONV)]
    if phase_buf:
        sbuf, = phase_buf
        for b in range(SUBLANES):
            mine = [o for o in offs if o % SUBLANES == b]
            if mine:
                n = max(mine) - b + tt
                sbuf[b, 0:n, :] = xbuf[b:b + n, :]
        tap = lambda o, r0: sbuf[o % SUBLANES, pl.ds(r0 + (o - o % SUBLANES), rb), :]
    else:
        tap = lambda o, r0: xbuf[pl.ds(r0 + o, rb), :]

    def row_block(i, carry):
        r0 = pl.multiple_of(i * rb, rb)
        acc = jnp.broadcast_to(cb_ref[...], (rb, ct))
        for k, o in enumerate(offs):
            acc = acc + cw_ref[k:k + 1, :] * tap(o, r0)
        out_ref[pl.ds(r0, rb), :] = acc
        return carry

    lax.fori_loop(0, tt // rb, row_block, 0)
    tail = xbuf[p0 + tt:p0 + tt + halo, :]
    bufnew_ref[0] = tail
    xbuf[p0:p0 + halo, :] = tail


def _conf_conv(u, buf0, cw, cb, *, n_seq, R, tt, ct):
    m = u.shape[0]
    nt = m // n_seq // tt
    nc = CONF_CH // ct
    halo = (CONF_CONV - 1) * R
    rows = _halo_pad(halo) + halo + tt
    rb = min(tt, CONV_ACC_VREGS * SUBLANES * LANES // ct)
    scratch = [pltpu.VMEM((rows, ct), F32)]
    if R % SUBLANES:
        scratch.append(pltpu.VMEM((SUBLANES, rows, ct), F32))
    return pl.pallas_call(
        functools.partial(_conf_kernel, R=R, tt=tt, ct=ct, rb=rb), name="conf_conv",
        grid=(n_seq, nc, nt),
        in_specs=[pl.BlockSpec((tt, ct), lambda s, c, i: (s * nt + i, c)),
                  pl.BlockSpec((tt, ct), lambda s, c, i: (s * nt + i, nc + c)),
                  pl.BlockSpec((1, halo, ct), lambda s, c, i: (s, 0, c)),
                  pl.BlockSpec((CONF_CONV, ct), lambda s, c, i: (0, c)),
                  pl.BlockSpec((1, ct), lambda s, c, i: (0, c))],
        out_specs=[pl.BlockSpec((tt, ct), lambda s, c, i: (s * nt + i, c)),
                   pl.BlockSpec((1, halo, ct), lambda s, c, i: (s, 0, c))],
        out_shape=[jax.ShapeDtypeStruct((m, CONF_CH), F32), jax.ShapeDtypeStruct((n_seq, halo, CONF_CH), F32)],
        scratch_shapes=scratch,
        compiler_params=_params("arbitrary", "arbitrary", "arbitrary"),
    )(u, u, buf0, cw, cb)


def _cmp_chunk_flat_kernel(kv_ref, pe_ref, w1_ref, f_ref, s_ref, *, tc):
    n_kv_rows = KV_W // HEAD_DIM
    for c in range(n_kv_rows):
        kind = c // N_KV
        xs = [kv_ref[pl.ds(l * n_kv_rows + c, tc, stride=CMP_STRIDE * n_kv_rows), :] for l in range(CMP_STRIDE)]
        for half, dst in ((0, f_ref), (1, s_ref)):
            xcat = jnp.concatenate(
                [(xs[l] + pe_ref[kind, half * CMP_STRIDE + l:half * CMP_STRIDE + l + 1, :]).astype(BF16)
                 for l in range(CMP_STRIDE)], axis=1)
            dst[:, c * HEAD_DIM:(c + 1) * HEAD_DIM] = jnp.dot(xcat, w1_ref[kind, half], preferred_element_type=F32)


def _cmp_chunks_flat(kv_flat, pe, w1, *, layer, n_pool):
    page_chunks = PAGE_SIZE // CMP_STRIDE
    nch = n_pool * page_chunks
    tc = page_chunks * max(d for d in range(1, 17) if n_pool % d == 0)
    steps = nch // tc
    rows = tc * CMP_STRIDE * (KV_W // HEAD_DIM)
    out = jax.ShapeDtypeStruct((nch, KV_W), F32)
    return pl.pallas_call(
        functools.partial(_cmp_chunk_flat_kernel, tc=tc), name="cmp_chunks_pool",
        grid=(steps,),
        in_specs=[pl.BlockSpec((rows, HEAD_DIM), lambda i: (layer * steps + i, 0)),
                  pl.BlockSpec((2, CMP_LEN, HEAD_DIM), lambda i: (0, 0, 0)),
                  pl.BlockSpec((2, 2, CMP_STRIDE * HEAD_DIM, CMP_HID), lambda i: (0, 0, 0, 0))],
        out_specs=[pl.BlockSpec((tc, KV_W), lambda i: (i, 0)), pl.BlockSpec((tc, KV_W), lambda i: (i, 0))],
        out_shape=[out, out],
        compiler_params=_params("arbitrary"),
    )(kv_flat, pe, w1)


def _cmp_blocks_body(f, s, w2_ref, kc_ref, vc_ref):
    n = f.shape[0]
    hid = _gelu(f + pltpu.roll(s, n - 1, axis=0)).astype(BF16)
    for kind, dst in ((0, kc_ref), (1, vc_ref)):
        outs = [jnp.dot(hid[:, (kind * N_KV + g) * CMP_HID:(kind * N_KV + g + 1) * CMP_HID], w2_ref[kind],
                        preferred_element_type=F32) for g in range(N_KV)]
        dst[0] = jnp.concatenate(outs, axis=1).astype(BF16)


def _cmp_blocks_kernel(f_ref, s_ref, w2_ref, kc_ref, vc_ref):
    _cmp_blocks_body(f_ref[0], s_ref[0], w2_ref, kc_ref, vc_ref)


def _cmp_blocks(f, s, w2):
    b, n, _ = f.shape
    out = jax.ShapeDtypeStruct((b, n, N_KV * HEAD_DIM), BF16)
    blk = pl.BlockSpec((1, n, KV_W), lambda i: (i, 0, 0))
    oblk = pl.BlockSpec((1, n, N_KV * HEAD_DIM), lambda i: (i, 0, 0))
    return pl.pallas_call(
        _cmp_blocks_kernel, name="cmp_blocks",
        grid=(b,), in_specs=[blk, blk, pl.BlockSpec((2, CMP_HID, HEAD_DIM), lambda i: (0, 0, 0))],
        out_specs=[oblk, oblk], out_shape=[out, out],
        compiler_params=_params("arbitrary"),
    )(f, s, w2)


def _cmp_blocks_paged_kernel(pt_ref, *refs, n_pages):
    f_refs, s_refs = refs[:n_pages], refs[n_pages:2 * n_pages]
    w2_ref, kc_ref, vc_ref = refs[2 * n_pages:]
    f = jnp.concatenate([r[0] for r in f_refs], axis=0)
    s = jnp.concatenate([r[0] for r in s_refs], axis=0)
    _cmp_blocks_body(f, s, w2_ref, kc_ref, vc_ref)


def _cmp_blocks_paged(f, s, w2, page_table):
    db, n_pages = page_table.shape
    cpp = f.shape[1]
    n = n_pages * cpp
    out = jax.ShapeDtypeStruct((db, n, N_KV * HEAD_DIM), BF16)
    page = lambda p: pl.BlockSpec((1, cpp, KV_W), lambda b, pt: (pt[b, p], 0, 0))
    oblk = pl.BlockSpec((1, n, N_KV * HEAD_DIM), lambda b, pt: (b, 0, 0))
    grid_spec = pltpu.PrefetchScalarGridSpec(
        num_scalar_prefetch=1, grid=(db,),
        in_specs=[page(p) for p in range(n_pages)] * 2 + [pl.BlockSpec((2, CMP_HID, HEAD_DIM), lambda b, pt: (0, 0, 0))],
        out_specs=[oblk, oblk])
    return pl.pallas_call(
        functools.partial(_cmp_blocks_paged_kernel, n_pages=n_pages), name="cmp_blocks_paged",
        grid_spec=grid_spec, out_shape=[out, out],
        compiler_params=_params("arbitrary"),
    )(page_table, *([f] * n_pages), *([s] * n_pages), w2)


def _stack_heads(q, g):
    return jnp.concatenate([q[:, (g * HPG + h) * HEAD_DIM:(g * HPG + h + 1) * HEAD_DIM] for h in range(HPG)], axis=0)


def _qk(q4, k):
    return lax.dot_general(q4, k, (((1,), (1,)), ((), ())), preferred_element_type=F32)


def _masked_attn(q4, k, v, valid, nq, probs=True):
    nk = k.shape[0]
    sm = jnp.where(valid[None], _qk(q4, k).reshape(HPG, nq, nk), MASKED)
    m = jnp.maximum(jnp.max(sm, axis=-1, keepdims=True), NEG)
    e = jnp.exp(sm - m)
    den = jnp.sum(e, axis=-1, keepdims=True)
    inv = 1.0 / jnp.where(den > 0.0, den, 1.0)
    if not probs:
        o = jnp.dot(e.reshape(HPG * nq, nk).astype(BF16), v, preferred_element_type=F32)
        return o * inv.reshape(HPG * nq, 1), None
    p = e * inv
    return jnp.dot(p.reshape(HPG * nq, nk).astype(BF16), v, preferred_element_type=F32), p


def _compressed_attn(q4, kc, vc, qpos, nq, n_cmp):
    ncp = kc.shape[0]
    ci = lax.broadcasted_iota(jnp.int32, (nq, ncp), 1)
    ok_c = (ci * CMP_STRIDE + (CMP_LEN - 1) <= qpos) & (ci < n_cmp)
    o_c, p = _masked_attn(q4, kc, vc, ok_c, nq)
    return o_c, jnp.sum(p, axis=0)


def _select_blocks(imp, msel_t, qpos_row):
    imp_s, rest = None, imp
    for _ in range(3):
        piece = rest.astype(BF16)
        rest = rest - piece.astype(F32)
        part = lax.dot_general(msel_t, piece, (((1,), (1,)), ((), ())), preferred_element_type=F32)
        imp_s = part if imp_s is None else imp_s + part
    blk = lax.broadcasted_iota(jnp.int32, (LANES, LANES), 0)
    cur = qpos_row // SEL_BLOCK
    visible = blk * SEL_BLOCK <= qpos_row
    forced = ((blk == 0) | (blk == cur) | (blk == cur - 1)) & visible
    sel = jnp.where(forced, 1.0, 0.0)
    score = jnp.where(visible & ~forced, imp_s, -BIG)
    for _ in range(N_SEL - N_FORCED):
        top = jnp.max(score, axis=0, keepdims=True)
        idx = jnp.min(jnp.where(score == top, blk, LANES), axis=0, keepdims=True)
        pick = blk == idx
        sel = jnp.where(pick & (top > -0.5 * BIG), 1.0, sel)
        score = jnp.where(pick, -3.0 * BIG, score)
    return sel.T


def _gate_mix(gates, o_c, o_s, o_w):
    outs = []
    for h in range(HPG):
        c0 = h * N_NSA_BRANCH
        outs.append(gates[:, c0:c0 + 1] * o_c[h] + gates[:, c0 + 1:c0 + 2] * o_s[h]
                    + gates[:, c0 + 2:c0 + 3] * o_w[h])
    return jnp.concatenate(outs, axis=1)


def _split_heads(o, nq):
    return [o[h * nq:(h + 1) * nq] for h in range(HPG)]


def _online_step(q4, k, v, valid, carry, nq):
    m, l, acc = carry
    nk = k.shape[0]
    sm = jnp.where(valid[None], _qk(q4, k).reshape(HPG, nq, nk), MASKED)
    m_new = jnp.maximum(m, jnp.max(sm, axis=-1, keepdims=True))
    alpha = jnp.exp(m - m_new)
    e = jnp.exp(sm - m_new)
    l = alpha * l + jnp.sum(e, axis=-1, keepdims=True)
    pv = jnp.dot(e.reshape(HPG * nq, nk).astype(BF16), v, preferred_element_type=F32)
    return m_new, l, alpha * acc + pv.reshape(HPG, nq, HEAD_DIM)


def _nsa_prompt_kernel(q_ref, kc_ref, vc_ref, ks_ref, vs_ref, kw_ref, vw_ref, gates_ref, msel_ref, expand_ref, o_ref,
                       *, seq, n_cmp, tk, wspan):
    j = pl.program_id(2)
    nq = Q_BLOCK
    start = j * Q_BLOCK
    qpos = start + lax.broadcasted_iota(jnp.int32, (nq, 1), 0)
    q4 = _stack_heads(q_ref[...], 0)
    o_c, imp = _compressed_attn(q4, kc_ref[0], vc_ref[0], qpos, nq, n_cmp)
    sel_b = _select_blocks(imp, msel_ref[...], start + lax.broadcasted_iota(jnp.int32, (1, nq), 1)).astype(BF16)

    def tile(kt, carry, diagonal):
        k0 = pl.multiple_of(kt * tk, tk)
        picked = jnp.dot(sel_b, expand_ref[:, pl.ds(k0, tk)], preferred_element_type=F32) > 0.5
        if diagonal:
            picked = picked & (k0 + lax.broadcasted_iota(jnp.int32, (nq, tk), 1) <= qpos)
        return _online_step(q4, ks_ref[pl.ds(k0, tk), :], vs_ref[pl.ds(k0, tk), :], picked, carry, nq)

    last = (start + Q_BLOCK - 1) // tk
    init = (jnp.full((HPG, nq, 1), NEG, F32), jnp.zeros((HPG, nq, 1), F32), jnp.zeros((HPG, nq, HEAD_DIM), F32))
    carry = lax.fori_loop(0, last, lambda kt, c: tile(kt, c, False), init)
    _, l, acc = tile(last, carry, True)
    o_s = (acc * (1.0 / jnp.where(l > 0.0, l, 1.0))).reshape(HPG * nq, HEAD_DIM)

    w0 = pl.multiple_of(jnp.clip(start - WINDOW, 0, seq - wspan), Q_BLOCK)
    kwpos = w0 + lax.broadcasted_iota(jnp.int32, (nq, wspan), 1)
    ok_w = (kwpos <= qpos) & (kwpos > qpos - WINDOW)
    o_w, _ = _masked_attn(q4, kw_ref[pl.ds(w0, wspan), :], vw_ref[pl.ds(w0, wspan), :], ok_w, nq, probs=False)
    o_ref[...] = _gate_mix(gates_ref[...], _split_heads(o_c, nq), _split_heads(o_s, nq), _split_heads(o_w, nq))


def _nsa_prompt(q, kc, vc, kvs_b, kvw_b, gates, msel, expand, *, batch, seq):
    m = q.shape[0]
    nj = seq // Q_BLOCK
    ncp = kc.shape[1]
    tk = min(SEL_KEY_TILE, seq)
    wspan = min(WINDOW + Q_BLOCK, seq)
    gw = HPG * HEAD_DIM
    kcol = lambda c: pl.BlockSpec((seq, HEAD_DIM), lambda b, g, j: (b, c * N_KV + g))
    return pl.pallas_call(
        functools.partial(_nsa_prompt_kernel, seq=seq, n_cmp=seq // CMP_STRIDE - 1, tk=tk, wspan=wspan),
        name="nsa_prompt",
        grid=(batch, N_KV, nj),
        in_specs=[pl.BlockSpec((Q_BLOCK, gw), lambda b, g, j: (b * nj + j, g)),
                  pl.BlockSpec((1, ncp, HEAD_DIM), lambda b, g, j: (b, 0, g)),
                  pl.BlockSpec((1, ncp, HEAD_DIM), lambda b, g, j: (b, 0, g)),
                  kcol(0), kcol(1), kcol(0), kcol(1),
                  pl.BlockSpec((Q_BLOCK, LANES), lambda b, g, j: (b * nj + j, g)),
                  pl.BlockSpec((LANES, ncp), lambda b, g, j: (0, 0)),
                  pl.BlockSpec((LANES, seq), lambda b, g, j: (0, 0))],
        out_specs=pl.BlockSpec((Q_BLOCK, gw), lambda b, g, j: (b * nj + j, g)),
        out_shape=jax.ShapeDtypeStruct((m, N_HEADS * HEAD_DIM), F32),
        compiler_params=_params("arbitrary", "arbitrary", "arbitrary"),
    )(q, kc, vc, kvs_b, kvs_b, kvw_b, kvw_b, gates, msel, expand)


def _nsa_sample_kernel(pt_ref, *refs, n_pages, past, wbuf, ts):
    page_refs = refs[:n_pages]
    (q_ref, kc_ref, vc_ref, ksn_ref, win_ref, kwn_ref, gates_ref, msel_ref, expand_ref,
     o_ref, wnew_ref) = refs[n_pages:]
    nq = SAMPLE_Q_PAD
    n_kv_rows = KV_W // HEAD_DIM
    n_cached = n_pages * PAGE_SIZE
    qpos = past + lax.broadcasted_iota(jnp.int32, (nq, 1), 0)
    zeros = jnp.zeros((PAGE_SIZE - nq, HEAD_DIM), BF16)
    kpos = lax.broadcasted_iota(jnp.int32, (nq, n_cached + PAGE_SIZE), 1)
    ok_s = (kpos <= qpos) & (kpos < past + nq)
    wpos = past - wbuf + lax.broadcasted_iota(jnp.int32, (nq, wbuf + PAGE_SIZE), 1)
    ok_w = (wpos <= qpos) & (wpos > qpos - WINDOW) & (wpos >= 0) & (wpos < past + nq)

    q4s, o_cs, imps = [], [], []
    for g in range(N_KV):
        gcols = slice(g * HEAD_DIM, (g + 1) * HEAD_DIM)
        q4s.append(_stack_heads(q_ref[...], g))
        o_c, imp = _compressed_attn(q4s[g], kc_ref[0][:, gcols], vc_ref[0][:, gcols], qpos, nq,
                                    n_cached // CMP_STRIDE - 1)
        o_cs.append(o_c)
        imps.append(imp)
    imp_all = jnp.concatenate(imps + [jnp.zeros((LANES - N_KV * nq, imps[0].shape[1]), F32)], axis=0)
    qpos_row = past + (lax.broadcasted_iota(jnp.int32, (1, LANES), 1) & (nq - 1))
    sel_all = _select_blocks(imp_all, msel_ref[...], qpos_row)

    def cached_rows(ref, n, c):
        return ref[pl.ds(c, n, stride=n_kv_rows), :].astype(BF16)

    def with_new(cached, new_ref, c):
        return jnp.concatenate(cached + [new_ref[:, c * HEAD_DIM:(c + 1) * HEAD_DIM].astype(BF16), zeros], axis=0)

    outs = []
    for g in range(N_KV):
        q4 = q4s[g]
        picked = jnp.dot(sel_all[g * nq:(g + 1) * nq].astype(BF16), expand_ref[...], preferred_element_type=F32) > 0.5
        k_all = with_new([cached_rows(r, PAGE_SIZE, g) for r in page_refs], ksn_ref, g)
        v_all = with_new([cached_rows(r, PAGE_SIZE, N_KV + g) for r in page_refs], ksn_ref, N_KV + g)
        o_s, _ = _masked_attn(q4, k_all, v_all, picked & ok_s, nq, probs=False)
        kw_all = with_new([cached_rows(win_ref, wbuf, g)], kwn_ref, g)
        vw_all = with_new([cached_rows(win_ref, wbuf, N_KV + g)], kwn_ref, N_KV + g)
        o_w, _ = _masked_attn(q4, kw_all, vw_all, ok_w, nq, probs=False)
        outs.append(_gate_mix(gates_ref[:, g * LANES:(g + 1) * LANES], _split_heads(o_cs[g], nq),
                              _split_heads(o_s, nq), _split_heads(o_w, nq)))
    o_ref[...] = jnp.concatenate(outs, axis=1)

    keep = (wbuf - ts) * n_kv_rows
    wnew_ref[0:keep, :] = win_ref[ts * n_kv_rows:wbuf * n_kv_rows, :]
    wnew_ref[keep:wbuf * n_kv_rows, :] = jnp.concatenate(
        [kwn_ref[t:t + 1, c * HEAD_DIM:(c + 1) * HEAD_DIM] for t in range(ts) for c in range(n_kv_rows)], axis=0)


def _nsa_sample(q8, kc, vc, slc_flat, kvs8, win_flat, kvw8, gates8, msel, expand, page_table, *,
                layer, n_pool, wbuf, past, ts):
    db, n_pages = page_table.shape
    nq = SAMPLE_Q_PAD
    n = kc.shape[1]
    n_kv_rows = KV_W // HEAD_DIM
    page = lambda p: pl.BlockSpec((PAGE_SIZE * n_kv_rows, HEAD_DIM), lambda b, pt: (layer * n_pool + pt[b, p], 0))
    rows = lambda w: pl.BlockSpec((nq, w), lambda b, pt: (b, 0))
    grid_spec = pltpu.PrefetchScalarGridSpec(
        num_scalar_prefetch=1, grid=(db,),
        in_specs=[page(p) for p in range(n_pages)] + [
            rows(N_HEADS * HEAD_DIM),
            pl.BlockSpec((1, n, N_KV * HEAD_DIM), lambda b, pt: (b, 0, 0)),
            pl.BlockSpec((1, n, N_KV * HEAD_DIM), lambda b, pt: (b, 0, 0)),
            rows(KV_W),
            pl.BlockSpec((wbuf * n_kv_rows, HEAD_DIM), lambda b, pt: (layer * db + b, 0)),
            rows(KV_W), rows(N_KV * LANES),
            pl.BlockSpec((LANES, n), lambda b, pt: (0, 0)),
            pl.BlockSpec((LANES, n_pages * PAGE_SIZE + PAGE_SIZE), lambda b, pt: (0, 0))],
        out_specs=[rows(N_HEADS * HEAD_DIM), pl.BlockSpec((wbuf * n_kv_rows, HEAD_DIM), lambda b, pt: (b, 0))])
    return pl.pallas_call(
        functools.partial(_nsa_sample_kernel, n_pages=n_pages, past=past, wbuf=wbuf, ts=ts), name="nsa_sample",
        grid_spec=grid_spec,
        out_shape=[jax.ShapeDtypeStruct((db * nq, N_HEADS * HEAD_DIM), F32),
                   jax.ShapeDtypeStruct((db * wbuf * n_kv_rows, HEAD_DIM), F32)],
        compiler_params=_params("arbitrary"),
    )(page_table, *([slc_flat] * n_pages), q8, kc, vc, kvs8, win_flat, kvw8, gates8, msel, expand)


def _merge_kernel(a_ref, c_ref, o_ref, mg_ref, x_ref, wb_ref, wo_ref, lng_ref, lnb_ref, gpost_ref, y_ref):
    c = c_ref[...]
    mu = jnp.mean(c, axis=-1, keepdims=True)
    var = jnp.mean(jnp.square(c - mu), axis=-1, keepdims=True)
    n = (c - mu) * lax.rsqrt(var + EPS) * lng_ref[...] + lnb_ref[...]
    branches = (a_ref[...], n * _sigmoid(n), o_ref[...])
    merged = None
    for i, br in enumerate(branches):
        proj = jnp.dot(br.astype(BF16), wb_ref[i], preferred_element_type=F32)
        term = mg_ref[:, i * D_MODEL:(i + 1) * D_MODEL] * proj
        merged = term if merged is None else merged + term
    mix = jnp.dot(merged.astype(BF16), wo_ref[...], preferred_element_type=F32)
    y_ref[...] = x_ref[...] + _rms(mix, gpost_ref[...])


def _merge(a, c, o, mg, x, wb, wo, lng, lnb, gpost, tm):
    m = x.shape[0]
    row = lambda w: pl.BlockSpec((tm, w), lambda i: (i, 0))
    vec = pl.BlockSpec((1, D_MODEL), lambda i: (0, 0))
    return pl.pallas_call(
        _merge_kernel, name="merge",
        grid=(m // tm,),
        in_specs=[row(D_MODEL), row(D_MODEL), row(D_MODEL), row(N_BRANCH * D_MODEL), row(D_MODEL),
                  pl.BlockSpec((N_BRANCH, LRU_WIDTH, D_MODEL), lambda i: (0, 0, 0)),
                  pl.BlockSpec((D_MODEL, D_MODEL), lambda i: (0, 0)), vec, vec, vec],
        out_specs=row(D_MODEL), out_shape=jax.ShapeDtypeStruct((m, D_MODEL), F32),
        compiler_params=_params("arbitrary"),
    )(a, c, o, mg, x, wb, wo, lng, lnb, gpost)


def _ffn_in_kernel(x_ref, g_ref, w_ref, gate_ref, up_ref, h_ref):
    j = pl.program_id(1)
    n_gate = D_FF // FFN_COL_TILE

    @pl.when(j == 0)
    def _():
        h_ref[...] = _rms(x_ref[...], g_ref[...]).astype(BF16)

    z = jnp.dot(h_ref[...], w_ref[...], preferred_element_type=F32)

    @pl.when(j < n_gate)
    def _():
        gate_ref[...] = z

    @pl.when(j >= n_gate)
    def _():
        up_ref[...] = z


def _ffn_in(x, g, w, tm):
    m = x.shape[0]
    n_gate = D_FF // FFN_COL_TILE
    out = jax.ShapeDtypeStruct((m, D_FF), F32)
    return pl.pallas_call(
        _ffn_in_kernel, name="ffn_in",
        grid=(m // tm, 2 * n_gate),
        in_specs=[pl.BlockSpec((tm, D_MODEL), lambda i, j: (i, 0)), pl.BlockSpec((1, D_MODEL), lambda i, j: (0, 0)),
                  pl.BlockSpec((D_MODEL, FFN_COL_TILE), lambda i, j: (0, j))],
        out_specs=[pl.BlockSpec((tm, FFN_COL_TILE), lambda i, j: (i, jnp.minimum(j, n_gate - 1))),
                   pl.BlockSpec((tm, FFN_COL_TILE), lambda i, j: (i, jnp.maximum(j - n_gate, 0)))],
        out_shape=[out, out],
        scratch_shapes=[pltpu.VMEM((tm, D_MODEL), BF16)],
        compiler_params=_params("arbitrary", "arbitrary"),
    )(x, g, w)


def _ffn_out_kernel(gate_ref, up_ref, x_ref, buf0_ref, cw_ref, cb_ref, wo_ref, gpost_ref, y_ref, bufnew_ref, gbuf,
                    *, R, tt):
    halo = (FFN_CONV - 1) * R
    p0 = _halo_pad(halo)

    @pl.when(pl.program_id(1) == 0)
    def _():
        gbuf[p0:p0 + halo, :] = buf0_ref[0]

    gbuf[p0 + halo:p0 + halo + tt, :] = gate_ref[...]
    c = jnp.broadcast_to(cb_ref[...], (tt, D_FF))
    for k in range(FFN_CONV):
        c = c + cw_ref[k:k + 1, :] * gbuf[p0 + k * R:p0 + k * R + tt, :]
    act = (_gelu(c) * up_ref[...]).astype(BF16)
    f = jnp.dot(act, wo_ref[...], preferred_element_type=F32)
    y_ref[...] = x_ref[...] + _rms(f, gpost_ref[...])
    tail = gbuf[p0 + tt:p0 + tt + halo, :]
    bufnew_ref[0] = tail
    gbuf[p0:p0 + halo, :] = tail


def _ffn_out(gate, up, x, buf0, cw, cb, wo, gpost, *, n_seq, R, tt):
    m = x.shape[0]
    nt = m // n_seq // tt
    halo = (FFN_CONV - 1) * R
    row = lambda w: pl.BlockSpec((tt, w), lambda s, i: (s * nt + i, 0))
    full2 = lambda s, i: (0, 0)
    seq = pl.BlockSpec((1, halo, D_FF), lambda s, i: (s, 0, 0))
    return pl.pallas_call(
        functools.partial(_ffn_out_kernel, R=R, tt=tt), name="ffn_out",
        grid=(n_seq, nt),
        in_specs=[row(D_FF), row(D_FF), row(D_MODEL), seq,
                  pl.BlockSpec((FFN_CONV, D_FF), full2), pl.BlockSpec((1, D_FF), full2),
                  pl.BlockSpec((D_FF, D_MODEL), full2), pl.BlockSpec((1, D_MODEL), full2)],
        out_specs=[row(D_MODEL), seq],
        out_shape=[jax.ShapeDtypeStruct((m, D_MODEL), F32), jax.ShapeDtypeStruct((n_seq, halo, D_FF), F32)],
        scratch_shapes=[pltpu.VMEM((_halo_pad(halo) + halo + tt, D_FF), F32)],
        compiler_params=_params("arbitrary", "arbitrary"),
    )(gate, up, x, buf0, cw, cb, wo, gpost)


def _rope_tables(pos):
    half = HEAD_DIM // 2
    inv = ROPE_THETA ** (-jnp.arange(half, dtype=F32) / half)
    ang = pos.astype(F32)[:, None] * inv[None, :]
    cos, sin = jnp.cos(ang), jnp.sin(ang)
    return jnp.concatenate([cos, cos], axis=1), jnp.concatenate([-sin, sin], axis=1)


def _select_matrix(ncp):
    i = np.arange(ncp)[None, :]
    s = np.arange(LANES)[:, None]
    return jnp.asarray((i // CH_PER_SEL == s).astype(np.float32) + ((i + 1) // CH_PER_SEL == s).astype(np.float32),
                       dtype=BF16)


def _expand_matrix(n_keys):
    return jnp.asarray(np.arange(n_keys)[None, :] // SEL_BLOCK == np.arange(LANES)[:, None], dtype=BF16)


def _layer_weights(l, norm_pre_mix, norm_post_mix, norm_pre_ffn, norm_post_ffn, w_in, lru_conv_w, lru_conv_b,
                   lru_gate_w, lru_gate_b, lru_lambda, conf_conv_w, conf_conv_b, conf_ln_g, conf_ln_b, cmp_pe,
                   cmp_w1, cmp_w2, w_branch, w_out, ffn_w_in, ffn_conv_w, ffn_conv_b, ffn_w_out):
    w = w_in[l]
    per_group = _NG_W // N_KV
    ng = jnp.concatenate([jnp.pad(w[:, _NG_OFF + g * per_group:_NG_OFF + (g + 1) * per_group],
                                  ((0, 0), (0, LANES - per_group))) for g in range(N_KV)], axis=1)
    ng = jnp.pad(ng, ((0, 0), (0, COL_TILE - N_KV * LANES)))
    w_perm = jnp.concatenate([w[:, :_NG_OFF], w[:, _NG_OFF + _NG_W:], ng], axis=1).astype(BF16)
    row = lambda v: v[l].reshape(1, -1)
    return dict(
        g_pre_mix=row(norm_pre_mix), g_post_mix=row(norm_post_mix), g_pre_ffn=row(norm_pre_ffn),
        g_post_ffn=row(norm_post_ffn), w_in=w_perm,
        lru_cw=lru_conv_w[l], lru_cb=row(lru_conv_b), lru_gw=lru_gate_w[l].astype(BF16),
        lru_gb=lru_gate_b[l].reshape(2, LRU_WIDTH), lru_lsl=jax.nn.log_sigmoid(lru_lambda[l].astype(F32)).reshape(1, -1),
        conf_cw=conf_conv_w[l], conf_cb=row(conf_conv_b), ln_g=row(conf_ln_g), ln_b=row(conf_ln_b),
        pe=cmp_pe[l], w1=cmp_w1[l].reshape(2, 2, CMP_STRIDE * HEAD_DIM, CMP_HID).astype(BF16),
        w2=cmp_w2[l].astype(BF16), wb=w_branch[l].astype(BF16), wo=w_out[l].astype(BF16),
        ffn_wi=ffn_w_in[l].astype(BF16), ffn_cw=ffn_conv_w[l], ffn_cb=row(ffn_conv_b),
        ffn_wo=ffn_w_out[l].astype(BF16))


def _layer(x, lw, cos, sin, state, attend, *, n_seq, R, tm, tt, conf_ct, flat_kv):
    lru_h0, lru_buf, conf_buf, ffn_buf = state
    xr, yr, u, q, kvc, kvs, kvw, kvs_b, kvw_b, mg, ng = _in_proj(x, lw["g_pre_mix"], lw["w_in"], cos, sin, tm, flat_kv)
    a_out, lru_h, lru_buf_new = _lru(xr, yr, lru_buf, lru_h0, lw["lru_cw"], lw["lru_cb"], lw["lru_gw"], lw["lru_gb"],
                                     lw["lru_lsl"], n_seq=n_seq, R=R, tt=tt)
    conv, conf_buf_new = _conf_conv(u, conf_buf, lw["conf_cw"], lw["conf_cb"], n_seq=n_seq, R=R,
                                    tt=x.shape[0] // n_seq if conf_ct < CONF_CH else tt, ct=conf_ct)
    o, win_new = attend(q, kvc, kvs, kvw, kvs_b, kvw_b, ng)
    x = _merge(a_out, conv, o, mg, x, lw["wb"], lw["wo"], lw["ln_g"], lw["ln_b"], lw["g_post_mix"], min(tm, 256))
    gate, up = _ffn_in(x, lw["g_pre_ffn"], lw["ffn_wi"], tm)
    x, ffn_buf_new = _ffn_out(gate, up, x, ffn_buf, lw["ffn_cw"], lw["ffn_cb"], lw["ffn_wo"], lw["g_post_ffn"],
                              n_seq=n_seq, R=R, tt=min(tt, 256) if R == 1 else R)
    return x, (kvc, kvs, win_new, lru_h, lru_buf_new, conf_buf_new, ffn_buf_new)


def kernel(x_prompt, x_sample, cache_cmp_kv, cache_slc_kv, cache_win_kv, state_lru_h, state_lru_conv, state_conf_conv, state_ffn_conv, page_table, norm_pre_mix, norm_post_mix, norm_pre_ffn, norm_post_ffn, w_in, lru_conv_w, lru_conv_b, lru_gate_w, lru_gate_b, lru_lambda, conf_conv_w, conf_conv_b, conf_ln_g, conf_ln_b, cmp_pe, cmp_w1, cmp_w2, w_branch, w_out, ffn_w_in, ffn_conv_w, ffn_conv_b, ffn_w_out):
    B, S, _ = x_prompt.shape
    DB, TS, _ = x_sample.shape
    depth = w_in.shape[0]
    n_pages = page_table.shape[1]
    past = n_pages * PAGE_SIZE
    n_pool = cache_cmp_kv.shape[1]
    wbuf = cache_win_kv.shape[2]
    assert S % SEL_KEY_TILE == 0 and TS <= SAMPLE_Q_PAD and DB % SUBLANES == 0 and wbuf == WINDOW
    kv_row = (2, N_KV, HEAD_DIM)
    cmp_flat = cache_cmp_kv.reshape(-1, HEAD_DIM)
    slc_flat = cache_slc_kv.reshape(-1, HEAD_DIM)
    win_flat = cache_win_kv.reshape(-1, HEAD_DIM)

    cos_p, sin_p = _rope_tables(jnp.arange(S))
    cos_s, sin_s = _rope_tables(past + jnp.repeat(jnp.arange(TS), DB))
    msel_p = _select_matrix(S // CMP_STRIDE)
    msel_s = _select_matrix(past // CMP_STRIDE)
    expand_p = _expand_matrix(S)
    expand_s = _expand_matrix(past + PAGE_SIZE)
    tm_p = 1024
    tm_s = DB * TS
    to_tb = lambda a: jnp.swapaxes(a, 0, 1).reshape((TS * DB,) + a.shape[2:])
    to_bt = lambda a: jnp.swapaxes(a.reshape((TS, DB) + a.shape[1:]), 0, 1)
    pad_q = lambda a: jnp.pad(to_bt(a), ((0, 0), (0, SAMPLE_Q_PAD - TS), (0, 0))).reshape(DB * SAMPLE_Q_PAD, -1)
    state_in = lambda a: jnp.swapaxes(a, 0, 1).reshape(1, -1, a.shape[-1])
    state_out = lambda a, k: jnp.swapaxes(a.reshape(k, DB, a.shape[-1]), 0, 1)

    xp = x_prompt.reshape(B * S, D_MODEL)
    xs = to_tb(x_sample)
    st_p, st_s = [], []
    for l in range(depth):
        lw = _layer_weights(l, norm_pre_mix, norm_post_mix, norm_pre_ffn, norm_post_ffn, w_in, lru_conv_w,
                            lru_conv_b, lru_gate_w, lru_gate_b, lru_lambda, conf_conv_w, conf_conv_b, conf_ln_g,
                            conf_ln_b, cmp_pe, cmp_w1, cmp_w2, w_branch, w_out, ffn_w_in, ffn_conv_w, ffn_conv_b,
                            ffn_w_out)

        def attend_p(q, kvc, kvs, kvw, kvs_b, kvw_b, ng, lw=lw):
            f, s = _cmp_chunks_flat(kvc, lw["pe"], lw["w1"], layer=0, n_pool=B * S // PAGE_SIZE)
            n = S // CMP_STRIDE
            kc, vc = _cmp_blocks(f.reshape(B, n, KV_W), s.reshape(B, n, KV_W), lw["w2"])
            o = _nsa_prompt(q, kc, vc, kvs_b, kvw_b, ng, msel_p, expand_p, batch=B, seq=S)
            return o, kvw.reshape((B, S) + kv_row)[:, S - min(WINDOW, S):]

        zeros = lambda k, c: jnp.zeros((B, k, c), F32)
        xp, sp = _layer(xp, lw, cos_p, sin_p,
                        (zeros(1, LRU_WIDTH), zeros(LRU_CONV - 1, LRU_WIDTH), zeros(CONF_CONV - 1, CONF_CH),
                         zeros(FFN_CONV - 1, D_FF)),
                        attend_p, n_seq=B, R=1, tm=tm_p, tt=256, conf_ct=CONF_CH, flat_kv=True)

        def attend_s(q, kvc, kvs, kvw, kvs_b, kvw_b, ng, lw=lw, l=l):
            cpp = PAGE_SIZE // CMP_STRIDE
            f, s = _cmp_chunks_flat(cmp_flat, lw["pe"], lw["w1"], layer=l, n_pool=n_pool)
            kc, vc = _cmp_blocks_paged(f.reshape(n_pool, cpp, KV_W), s.reshape(n_pool, cpp, KV_W), lw["w2"],
                                       page_table)
            o8, win_new = _nsa_sample(pad_q(q), kc, vc, slc_flat, pad_q(kvs), win_flat, pad_q(kvw), pad_q(ng),
                                      msel_s, expand_s, page_table, layer=l, n_pool=n_pool, wbuf=wbuf, past=past,
                                      ts=TS)
            return to_tb(o8.reshape(DB, SAMPLE_Q_PAD, -1)[:, :TS]), win_new.reshape((DB, wbuf) + kv_row)

        xs, ss = _layer(xs, lw, cos_s, sin_s,
                        (state_lru_h[l][None], state_in(state_lru_conv[l]), state_in(state_conf_conv[l]),
                         state_in(state_ffn_conv[l])),
                        attend_s, n_seq=1, R=DB, tm=tm_s, tt=DB, conf_ct=256, flat_kv=False)
        st_p.append(sp)
        st_s.append(ss)

    def kv_p(i):
        return jnp.stack([s[i].reshape((B, S) + kv_row) for s in st_p])

    def kv_s(i):
        return jnp.stack([to_bt(s[i]).reshape((DB, TS) + kv_row) for s in st_s])

    return (xp.reshape(B, S, D_MODEL), to_bt(xs),
            kv_p(0), kv_s(0), kv_p(1), kv_s(1), jnp.stack([s[2] for s in st_p]), jnp.stack([s[2] for s in st_s]),
            jnp.stack([s[3][:, 0] for s in st_p]), jnp.stack([s[3][0] for s in st_s]),
            jnp.stack([s[4] for s in st_p]), jnp.stack([state_out(s[4], LRU_CONV - 1) for s in st_s]),
            jnp.stack([s[5] for s in st_p]), jnp.stack([state_out(s[5], CONF_CONV - 1) for s in st_s]),
            jnp.stack([s[6] for s in st_p]), jnp.stack([state_out(s[6], FFN_CONV - 1) for s in st_s]))
```
